```python
import math
import jax, jax.numpy as jnp
from jax import lax
import numpy as np

D_MODEL = 1024
BATCH = 8
SEQ = 2048
DEPTH = 1
DEC_BATCH = 128
DEC_SEQ = 4
PAST_LEN = 16384
PAGE_SIZE = 128

D_PLE = 256
MIX_A = 512
MIX_B = 512
N_HEADS_A = 4
HEAD_DIM_A = MIX_A // N_HEADS_A
CONV_W = 4
CHUNK_A = 64
S5_GROUP = 16
N_GROUPS_B = MIX_B // S5_GROUP
S5_STATE = 64
EPS = 1e-6
IN_COLS = 3 * MIX_A + 2 * N_HEADS_A + 2 * MIX_B
SPLIT_IDX = (MIX_A, 2 * MIX_A, 3 * MIX_A, 3 * MIX_A + N_HEADS_A,
             3 * MIX_A + 2 * N_HEADS_A, 3 * MIX_A + 2 * N_HEADS_A + MIX_B)

kernel_name = 'hymba_mlstm_s5_decode_step'


def rmsnorm(x, g):
    xf = x.astype(jnp.float32)
    y = xf * lax.rsqrt(jnp.mean(xf * xf, axis=-1, keepdims=True) + EPS)
    return (y * g.astype(jnp.float32)).astype(x.dtype)


def causal_conv(x, buf, w, b):
    L = x.shape[1]
    xp = jnp.concatenate([buf, x], axis=1)
    y = b.astype(jnp.float32)
    for j in range(CONV_W):
        y = y + xp[:, j:j + L] * w[j].astype(jnp.float32)
    return y, xp[:, L:]


def mlstm_chunkwise(q, k, v, i_t, logf, C0, n0, m0):
    Bn, L, H, d = q.shape
    c = math.gcd(L, CHUNK_A)
    nc = L // c

    def to_chunks(t):
        t = t.reshape((Bn, nc, c) + t.shape[2:])
        return jnp.moveaxis(t, (1, 3), (0, 2))

    causal = jnp.tril(jnp.ones((c, c), dtype=bool))

    def step(carry, xs):
        C, n, m = carry
        qc, kc, vc, ic, fc = xs
        b = jnp.cumsum(fc, axis=-1)
        dmat = b[..., :, None] - b[..., None, :] + ic[..., None, :]
        dmat = jnp.where(causal, dmat, -jnp.inf)
        m_inter = b + m[..., None]
        m_t = jnp.maximum(m_inter, jnp.max(dmat, axis=-1))
        s = jnp.einsum('bhtk,bhsk->bhts', qc, kc) * jnp.exp(dmat - m_t[..., None])
        decay = jnp.exp(m_inter - m_t)
        num = jnp.einsum('bhts,bhsv->bhtv', s, vc) + decay[..., None] * jnp.einsum('bhtk,bhkv->bhtv', qc, C)
        qn = jnp.sum(s, axis=-1) + decay * jnp.einsum('bhtk,bhk->bht', qc, n)
        h = num / jnp.maximum(jnp.abs(qn), jnp.exp(-m_t))[..., None]
        m_new = m_t[..., -1]
        wl = jnp.exp(b[..., -1:] - b + ic - m_new[..., None])
        dl = jnp.exp(b[..., -1] + m - m_new)
        C_new = dl[..., None, None] * C + jnp.einsum('bhs,bhsk,bhsv->bhkv', wl, kc, vc)
        n_new = dl[..., None] * n + jnp.einsum('bhs,bhsk->bhk', wl, kc)
        return (C_new, n_new, m_new), h

    xs = (to_chunks(q), to_chunks(k), to_chunks(v), to_chunks(i_t), to_chunks(logf))
    (C1, n1, m1), h = lax.scan(step, (C0, n0, m0), xs)
    h = jnp.moveaxis(h, (0, 2), (1, 3)).reshape(Bn, L, H, d)
    return h, C1, n1, m1


def _cplx_combine(e1, e2):
    a1r, a1i, b1r, b1i = e1
    a2r, a2i, b2r, b2i = e2
    return (a2r * a1r - a2i * a1i, a2r * a1i + a2i * a1r,
            a2r * b1r - a2i * b1i + b2r, a2r * b1i + a2i * b1r + b2i)


def s5_scan(u, h0_re, h0_im, lam_re, lam_im, log_dt, B_re, B_im, C_re, C_im, d_skip):
    f32 = jnp.float32
    Bn, L, _ = u.shape
    ug = u.reshape(Bn, L, N_GROUPS_B, S5_GROUP)
    lr = jnp.minimum(lam_re.astype(f32), -1e-4)
    li = lam_im.astype(f32)
    dt = jnp.exp(log_dt.astype(f32))
    mag = jnp.exp(lr * dt)
    a_re = mag * jnp.cos(li * dt)
    a_im = mag * jnp.sin(li * dt)
    den = lr * lr + li * li
    xr = a_re - 1.0
    g_re = (xr * lr + a_im * li) / den
    g_im = (a_im * lr - xr * li) / den
    Br = B_re.astype(f32)
    Bi = B_im.astype(f32)
    Bb_re = g_re[..., None] * Br - g_im[..., None] * Bi
    Bb_im = g_re[..., None] * Bi + g_im[..., None] * Br
    bu_re = jnp.einsum('blgc,gpc->blgp', ug, Bb_re)
    bu_im = jnp.einsum('blgc,gpc->blgp', ug, Bb_im)
    h0r = h0_re.astype(f32)
    h0i = h0_im.astype(f32)
    bu_re = bu_re.at[:, 0].add(a_re * h0r - a_im * h0i)
    bu_im = bu_im.at[:, 0].add(a_re * h0i + a_im * h0r)
    ar = jnp.broadcast_to(a_re, bu_re.shape)
    ai = jnp.broadcast_to(a_im, bu_re.shape)
    _, _, xs_re, xs_im = lax.associative_scan(_cplx_combine, (ar, ai, bu_re, bu_im), axis=1)
    y = (jnp.einsum('blgp,gcp->blgc', xs_re, C_re.astype(f32))
         - jnp.einsum('blgp,gcp->blgc', xs_im, C_im.astype(f32)))
    y = y.reshape(Bn, L, MIX_B) + d_skip.astype(f32) * u
    return y, xs_re[:, -1], xs_im[:, -1]


def hybrid_layer(h, p_l, C0, n0, m0, conv0, sre0, sim0,
                 ln_mix, w_in, b_igate, b_fgate, conv_w, conv_b, w_q, w_k, w_v, ln_head, skip_a,
                 lam_re, lam_im, log_dt, B_re, B_im, C_re, C_im, s5_D, w_glu, b_glu,
                 w_out, w_ple, ln_ple, w_ple_gate):
    f32 = jnp.float32
    Bn, L, _ = h.shape
    a = rmsnorm(h, ln_mix)
    proj = a @ w_in
    x_m, z_m, o_m, i_pre, f_pre, x_s, z_s = jnp.split(proj, SPLIT_IDX, axis=-1)

    x_m = x_m.astype(f32)
    x_conv, conv_new = causal_conv(x_m, conv0.astype(f32), conv_w, conv_b)
    x_conv = jax.nn.silu(x_conv)
    xc_h = x_conv.reshape(Bn, L, N_HEADS_A, HEAD_DIM_A)
    xm_h = x_m.reshape(Bn, L, N_HEADS_A, HEAD_DIM_A)
    q = jnp.einsum('blhd,hde->blhe', xc_h, w_q.astype(f32))
    k = jnp.einsum('blhd,hde->blhe', xc_h, w_k.astype(f32)) * (HEAD_DIM_A ** -0.5)
    v = jnp.einsum('blhd,hde->blhe', xm_h, w_v.astype(f32))
    i_t = i_pre.astype(f32) + b_igate.astype(f32)
    logf = jax.nn.log_sigmoid(f_pre.astype(f32) + b_fgate.astype(f32))
    hA, C1, n1, m1 = mlstm_chunkwise(q, k, v, i_t, logf, C0.astype(f32), n0.astype(f32), m0.astype(f32))
    hA = hA * jax.nn.sigmoid(o_m.astype(f32)).reshape(Bn, L, N_HEADS_A, HEAD_DIM_A)
    mu = jnp.mean(hA, axis=-1, keepdims=True)
    var = jnp.mean(jnp.square(hA - mu), axis=-1, keepdims=True)
    hA = ((hA - mu) * lax.rsqrt(var + EPS)).reshape(Bn, L, MIX_A) * ln_head.astype(f32)
    hA = (hA + skip_a.astype(f32) * x_conv) * jax.nn.silu(z_m.astype(f32))

    yB, sre1, sim1 = s5_scan(x_s.astype(f32), sre0, sim0, lam_re, lam_im, log_dt,
                             B_re, B_im, C_re, C_im, s5_D)
    yB = jax.nn.gelu(yB)
    yB = yB * jax.nn.sigmoid(yB @ w_glu.astype(f32) + b_glu.astype(f32))
    yB = yB * jax.nn.silu(z_s.astype(f32))

    h = h + jnp.concatenate([hA, yB], axis=-1).astype(h.dtype) @ w_out
    e = rmsnorm(p_l @ w_ple, ln_ple)
    h = h + jax.nn.sigmoid(h @ w_ple_gate) * e
    return h, C1, n1, m1, conv_new, sre1, sim1


def setup_inputs(seed: int = 0) -> dict:
    key = jax.random.key(seed)
    ks = iter(jax.random.split(key, 48))
    f32 = jnp.float32

    def nrm(shape, s):
        return s * jax.random.normal(next(ks), shape, f32)

    H, dh, G, P = N_HEADS_A, HEAD_DIM_A, N_GROUPS_B, S5_STATE
    inp = {}
    inp['x_prompt'] = nrm((BATCH, SEQ, D_MODEL), 1.0)
    inp['x_sample'] = nrm((DEC_BATCH, DEC_SEQ, D_MODEL), 1.0)
    inp['p_prompt'] = nrm((DEPTH, BATCH, SEQ, D_PLE), 1.0)
    inp['p_sample'] = nrm((DEPTH, DEC_BATCH, DEC_SEQ, D_PLE), 1.0)
    inp['state_mlstm_C'] = nrm((DEPTH, DEC_BATCH, H, dh, dh), 0.1)
    inp['state_mlstm_n'] = nrm((DEPTH, DEC_BATCH, H, dh), 0.5)
    inp['state_mlstm_m'] = nrm((DEPTH, DEC_BATCH, H), 1.0)
    inp['state_conv'] = nrm((DEPTH, DEC_BATCH, CONV_W - 1, MIX_A), 1.0)
    inp['state_s5_re'] = nrm((DEPTH, DEC_BATCH, G, P), 0.3)
    inp['state_s5_im'] = nrm((DEPTH, DEC_BATCH, G, P), 0.3)
    inp['ln_mix'] = 1.0 + nrm((DEPTH, D_MODEL), 0.02)
    inp['w_in'] = nrm((DEPTH, D_MODEL, IN_COLS), D_MODEL ** -0.5)
    inp['b_igate'] = nrm((DEPTH, H), 0.1)
    inp['b_fgate'] = jnp.linspace(3.0, 6.0, H, dtype=f32)[None, :] + nrm((DEPTH, H), 0.1)
    inp['conv_w'] = nrm((DEPTH, CONV_W, MIX_A), CONV_W ** -0.5)
    inp['conv_b'] = nrm((DEPTH, MIX_A), 0.02)
    inp['w_q'] = nrm((DEPTH, H, dh, dh), dh ** -0.5)
    inp['w_k'] = nrm((DEPTH, H, dh, dh), dh ** -0.5)
    inp['w_v'] = nrm((DEPTH, H, dh, dh), dh ** -0.5)
    inp['ln_head'] = 1.0 + nrm((DEPTH, MIX_A), 0.02)
    inp['skip_a'] = 1.0 + nrm((DEPTH, MIX_A), 0.02)
    inp['s5_lam_re'] = -0.5 + nrm((DEPTH, G, P), 0.01)
    inp['s5_lam_im'] = math.pi * jnp.arange(P, dtype=f32)[None, None, :] + nrm((DEPTH, G, P), 0.01)
    lo, hi = math.log(0.001), math.log(0.1)
    inp['s5_log_dt'] = lo + (hi - lo) * jax.random.uniform(next(ks), (DEPTH, G, P), f32)
    inp['s5_B_re'] = nrm((DEPTH, G, P, S5_GROUP), (2 * S5_GROUP) ** -0.5)
    inp['s5_B_im'] = nrm((DEPTH, G, P, S5_GROUP), (2 * S5_GROUP) ** -0.5)
    inp['s5_C_re'] = nrm((DEPTH, G, S5_GROUP, P), 0.5)
    inp['s5_C_im'] = nrm((DEPTH, G, S5_GROUP, P), 0.5)
    inp['s5_D'] = nrm((DEPTH, MIX_B), 1.0)
    inp['w_glu'] = nrm((DEPTH, MIX_B, MIX_B), MIX_B ** -0.5)
    inp['b_glu'] = nrm((DEPTH, MIX_B), 0.02)
    inp['w_out'] = nrm((DEPTH, MIX_A + MIX_B, D_MODEL), (MIX_A + MIX_B) ** -0.5)
    inp['w_ple'] = nrm((DEPTH, D_PLE, D_MODEL), D_PLE ** -0.5)
    inp['ln_ple'] = 1.0 + nrm((DEPTH, D_MODEL), 0.02)
    inp['w_ple_gate'] = nrm((DEPTH, D_MODEL, D_MODEL), D_MODEL ** -0.5)
    inp['ln_final'] = 1.0 + nrm((D_MODEL,), 0.02)
    return inp


def reference(x_prompt, x_sample, p_prompt, p_sample, state_mlstm_C, state_mlstm_n, state_mlstm_m,
              state_conv, state_s5_re, state_s5_im, ln_mix, w_in, b_igate, b_fgate, conv_w, conv_b,
              w_q, w_k, w_v, ln_head, skip_a, s5_lam_re, s5_lam_im, s5_log_dt, s5_B_re, s5_B_im,
              s5_C_re, s5_C_im, s5_D, w_glu, b_glu, w_out, w_ple, ln_ple, w_ple_gate, ln_final):
    f32 = jnp.float32
    H, dh, G, P = N_HEADS_A, HEAD_DIM_A, N_GROUPS_B, S5_STATE
    hp, hs = x_prompt, x_sample
    pr = [[] for _ in range(6)]
    sa = [[] for _ in range(6)]
    for l in range(DEPTH):
        wl = (ln_mix[l], w_in[l], b_igate[l], b_fgate[l], conv_w[l], conv_b[l], w_q[l], w_k[l], w_v[l],
              ln_head[l], skip_a[l], s5_lam_re[l], s5_lam_im[l], s5_log_dt[l], s5_B_re[l], s5_B_im[l],
              s5_C_re[l], s5_C_im[l], s5_D[l], w_glu[l], b_glu[l], w_out[l], w_ple[l], ln_ple[l],
              w_ple_gate[l])
        Bp = hp.shape[0]
        outp = hybrid_layer(hp, p_prompt[l],
                            jnp.zeros((Bp, H, dh, dh), f32), jnp.zeros((Bp, H, dh), f32),
                            jnp.zeros((Bp, H), f32), jnp.zeros((Bp, CONV_W - 1, MIX_A), f32),
                            jnp.zeros((Bp, G, P), f32), jnp.zeros((Bp, G, P), f32), *wl)
        outs = hybrid_layer(hs, p_sample[l], state_mlstm_C[l], state_mlstm_n[l], state_mlstm_m[l],
                            state_conv[l], state_s5_re[l], state_s5_im[l], *wl)
        hp, hs = outp[0], outs[0]
        for j in range(6):
            pr[j].append(outp[j + 1])
            sa[j].append(outs[j + 1])
    y_prompt = rmsnorm(hp, ln_final)
    y_sample = rmsnorm(hs, ln_final)
    pC, pn, pm, pconv, pre, pim = [jnp.stack(t, axis=0) for t in pr]
    sC, sn, sm, sconv, sre, sim = [jnp.stack(t, axis=0) for t in sa]
    return (y_prompt, y_sample, pC, pn, pm, pconv, pre, pim, sC, sn, sm, sconv, sre, sim)
```

```python
import functools
import math

import jax
import jax.numpy as jnp
from jax import lax
from jax.experimental import pallas as pl
from jax.experimental.pallas import tpu as pltpu

F32 = jnp.float32
BF16 = jnp.bfloat16

D_MODEL = 1024
D_PLE = 256
MIX = 512
N_HEADS = 4
HEAD_DIM = 128
N_GROUPS = 32
S5_GROUP = 16
S5_STATE = 64
N_STATE = N_GROUPS * S5_STATE
CONV_W = 4
EPS = 1e-6
K_SCALE = HEAD_DIM ** -0.5
ROWS = 512
GROUP_ROWS = 256
VMEM_LIMIT = 56 * 1024 * 1024


def _mm(a, w):
    return jnp.dot(a, w, preferred_element_type=F32)


def _rms(x, g):
    ms = jnp.mean(x * x, axis=-1, keepdims=True)
    return x * lax.rsqrt(ms + EPS) * g


def _silu(x):
    return x * jax.nn.sigmoid(x)


def _gelu_tanh(x):
    c = math.sqrt(2.0 / math.pi)
    return 0.5 * x * (1.0 + jnp.tanh(c * (x + 0.044715 * (x * x * x))))


def _log_sigmoid(x):
    return jnp.minimum(x, 0.0) - jnp.log1p(jnp.exp(-jnp.abs(x)))


def _seg_scan(x, tidx, n_t, t_stride, op, fill):
    sh = 1
    while sh < n_t:
        prev = pltpu.roll(x, sh * t_stride, axis=1)
        x = op(x, jnp.where(tidx >= sh, prev, fill))
        sh *= 2
    return x


def _bcast_last(x, tidx, n_t, t_stride):
    y = jnp.where(tidx == n_t - 1, x, 0.0)
    sh = 1
    while sh < n_t:
        nxt = pltpu.roll(y, ROWS - sh * t_stride, axis=1)
        y = y + jnp.where(tidx + sh <= n_t - 1, nxt, 0.0)
        sh *= 2
    return y


def _gate_rows(g_cols, bi8, bf8, mprev, n_t, t_stride):
    gt = g_cols.T
    lane = lax.broadcasted_iota(jnp.int32, (8, ROWS), 1)
    tidx = (lane // t_stride) % n_t
    i8 = gt[0:8] + bi8
    logf = _log_sigmoid(gt[8:16] + bf8)
    b = _seg_scan(logf, tidx, n_t, t_stride, jnp.add, 0.0)
    c = i8 - b
    cm = _seg_scan(c, tidx, n_t, t_stride, jnp.maximum, -jnp.inf)
    big_m = jnp.maximum(mprev, cm)
    mt = b + big_m
    m_last = _bcast_last(big_m, tidx, n_t, t_stride)
    mnew = _bcast_last(mt, tidx, n_t, t_stride)
    stack = jnp.concatenate(
        [-big_m, jnp.exp(mprev - big_m), jnp.exp(-mt), jnp.exp(c - m_last), jnp.exp(mprev - m_last),
         jnp.zeros((128 - 40, ROWS), F32)], axis=0)
    return c, stack.T, mnew


COL_NEGM, COL_DECAY, COL_EM, COL_WL, COL_DL = 0, 8, 16, 24, 32


def _intra(qg, kg, vg, col_a, row_b, mask):
    s = lax.dot_general(qg, kg, (((1,), (1,)), ((), ())), preferred_element_type=F32)
    p = s * jnp.exp(jnp.where(mask, col_a + row_b, -jnp.inf))
    rs = jnp.sum(p, axis=-1, keepdims=True)
    return _mm(p.astype(BF16), vg), rs


def _head_norm(x):
    mu = jnp.mean(x, axis=-1, keepdims=True)
    xc = x - mu
    var = jnp.mean(xc * xc, axis=-1, keepdims=True)
    return xc * lax.rsqrt(var + EPS)


def _s5_scan(xsre_s, xsim_s, are_ref, aim_ref, sre_ref, sim_ref, row_base, st_rows, n_t):
    chunk = 512
    for ch in range(N_STATE // chunk):
        ls = slice(ch * chunk, (ch + 1) * chunk)
        ar = are_ref[:, ls]
        ai = aim_ref[:, ls]

        def body(t, carry, ls=ls, ar=ar, ai=ai):
            xr, xi = carry
            r0 = pl.multiple_of(row_base + t * 8, 8)
            nr = ar * xr - ai * xi + xsre_s[pl.ds(r0, 8), ls]
            ni = ar * xi + ai * xr + xsim_s[pl.ds(r0, 8), ls]
            xsre_s[pl.ds(r0, 8), ls] = nr
            xsim_s[pl.ds(r0, 8), ls] = ni
            return nr, ni

        xr, xi = lax.fori_loop(0, n_t, body, (sre_ref[st_rows, ls], sim_ref[st_rows, ls]),
                               unroll=min(n_t, 8))
        sre_ref[st_rows, ls] = xr
        sim_ref[st_rows, ls] = xi


def _s5_in(u_tm, wb_ref, xsre_s, xsim_s):
    half = N_STATE // 2
    for hf in range(2):
        res = _mm(u_tm[:, hf * 256:(hf + 1) * 256].astype(BF16), wb_ref[hf])
        xsre_s[:, hf * half:(hf + 1) * half] = res[:, :half]
        xsim_s[:, hf * half:(hf + 1) * half] = res[:, half:]


def _s5_out(xsre_s, xsim_s, wcre_ref, wcim_ref, hf):
    half = N_STATE // 2
    hs = slice(hf * half, (hf + 1) * half)
    return (_mm(xsre_s[:, hs].astype(BF16), wcre_ref[hf])
            - _mm(xsim_s[:, hs].astype(BF16), wcim_ref[hf]))


def _s5_post(y_s5, u, a, wbig_ref, wglu_ref, s5d_ref, bglu_ref):
    yb = _gelu_tanh(y_s5 + s5d_ref[...] * u)
    yb = yb * jax.nn.sigmoid(_mm(yb.astype(BF16), wglu_ref[...]) + bglu_ref[...])
    return yb * _silu(_mm(a, wbig_ref[:, 2048:2560]))


def _merge_out(x, p, h_a, y_b, wout_ref, wple_ref, wpg_ref, lnple_ref, lnfin_ref):
    h1 = x + _mm(h_a.astype(BF16), wout_ref[0:MIX, :]) + _mm(y_b.astype(BF16), wout_ref[MIX:2 * MIX, :])
    e = _rms(_mm(p.astype(BF16), wple_ref[...]), lnple_ref[...])
    h2 = h1 + jax.nn.sigmoid(_mm(h1.astype(BF16), wpg_ref[...])) * e
    return _rms(h2, lnfin_ref[...])


def _prep_body(lr_ref, li_ref, ldt_ref, btr_ref, bti_ref, are_ref, aim_ref, wb_ref):
    lr = jnp.minimum(lr_ref[...], -1e-4)
    li = li_ref[...]
    dt = jnp.exp(ldt_ref[...])
    mag = jnp.exp(lr * dt)
    a_re = mag * jnp.cos(li * dt)
    a_im = mag * jnp.sin(li * dt)
    den = lr * lr + li * li
    xr = a_re - 1.0
    g_re = (xr * lr + a_im * li) / den
    g_im = (a_im * lr - xr * li) / den
    are_ref[...] = jnp.broadcast_to(a_re, (8, N_STATE))
    aim_ref[...] = jnp.broadcast_to(a_im, (8, N_STATE))
    half = N_STATE // 2
    row = lax.broadcasted_iota(jnp.int32, (256, half), 0)
    col = lax.broadcasted_iota(jnp.int32, (256, half), 1)
    mask = (row // S5_GROUP) == (col // S5_STATE)
    for hf in range(2):
        gr = g_re[:, hf * half:(hf + 1) * half]
        gi = g_im[:, hf * half:(hf + 1) * half]
        br = btr_ref[hf]
        bi = bti_ref[hf]
        wb_ref[hf, :, 0:half] = jnp.where(mask, gr * br - gi * bi, 0.0).astype(BF16)
        wb_ref[hf, :, half:2 * half] = jnp.where(mask, gr * bi + gi * br, 0.0).astype(BF16)


def _prep_call(lam_re, lam_im, log_dt, bt_re, bt_im):
    return pl.pallas_call(
        _prep_body,
        out_shape=(jax.ShapeDtypeStruct((8, N_STATE), F32), jax.ShapeDtypeStruct((8, N_STATE), F32),
                   jax.ShapeDtypeStruct((2, 256, N_STATE), BF16)),
        name="s5_prep",
    )(lam_re, lam_im, log_dt, bt_re, bt_im)


P_T = 64
P_NB = 8


def _prompt_body(x_ref, p_ref, wbig_ref, wgate_ref, wqk_ref, wv_ref, wb_ref, wcre_ref, wcim_ref,
                 wglu_ref, wout_ref, wple_ref, wpg_ref, lnmix_ref, convw_ref, convb_ref, bi8_ref, bf8_ref,
                 lnhead_ref, skip_ref, s5d_ref, bglu_ref, lnple_ref, lnfin_ref, are_ref, aim_ref,
                 y_ref, c_ref, n_ref, mrow_ref, conv_ref, sre_ref, sim_ref,
                 ext_s, ha_s, utm_s, ytm_s, ybm_s, xsre_s, xsim_s):
    step = pl.program_id(0)

    @pl.when(step == 0)
    def _init():
        c_ref[...] = jnp.zeros_like(c_ref)
        n_ref[...] = jnp.zeros_like(n_ref)
        mrow_ref[...] = jnp.zeros_like(mrow_ref)
        sre_ref[...] = jnp.zeros_like(sre_ref)
        sim_ref[...] = jnp.zeros_like(sim_ref)
        ext_s[:, 0:8, :] = jnp.zeros((P_NB, 8, MIX), F32)

    x = x_ref[...].reshape(ROWS, D_MODEL)
    a = _rms(x, lnmix_ref[...]).astype(BF16)

    xm = _mm(a, wbig_ref[:, 0:512])
    ext_s[:, 8:8 + P_T, :] = xm.reshape(P_NB, P_T, MIX)
    yc = convb_ref[...]
    for j in range(CONV_W):
        yc = yc + ext_s[:, 5 + j:5 + j + P_T, :].reshape(ROWS, MIX) * convw_ref[j:j + 1, :]
    tail = ext_s[:, 5 + P_T:8 + P_T, :]
    conv_ref[0] = tail
    ext_s[:, 5:8, :] = tail
    xc = _silu(yc)
    xc_bf = xc.astype(BF16)
    xm_bf = xm.astype(BF16)

    c8, colt, mnew = _gate_rows(_mm(a, wgate_ref[...]), bi8_ref[...], bf8_ref[...], mrow_ref[...], P_T, 1)
    mrow_ref[...] = mnew

    r = lax.broadcasted_iota(jnp.int32, (GROUP_ROWS, GROUP_ROWS), 0)
    cc = lax.broadcasted_iota(jnp.int32, (GROUP_ROWS, GROUP_ROWS), 1)
    mask = ((r // P_T) == (cc // P_T)) & (cc <= r)

    for h in range(N_HEADS):
        hs = slice(h * HEAD_DIM, (h + 1) * HEAD_DIM)
        qk = _mm(xc_bf[:, hs], wqk_ref[h])
        q_f = qk[:, :HEAD_DIM]
        k_f = qk[:, HEAD_DIM:] * K_SCALE
        v_f = _mm(xm_bf[:, hs], wv_ref[h])
        q_bf = q_f.astype(BF16)
        k_bf = k_f.astype(BF16)
        v_bf = v_f.astype(BF16)
        for g in range(ROWS // GROUP_ROWS):
            gs = slice(g * GROUP_ROWS, (g + 1) * GROUP_ROWS)
            decay = colt[gs, COL_DECAY + h:COL_DECAY + h + 1]
            wl = colt[gs, COL_WL + h:COL_WL + h + 1]
            dl = colt[gs, COL_DL + h:COL_DL + h + 1]
            num, rs = _intra(q_bf[gs], k_bf[gs], v_bf[gs], colt[gs, COL_NEGM + h:COL_NEGM + h + 1],
                             c8[h:h + 1, gs], mask)
            wlv = (wl * v_f[gs]).astype(BF16)
            wlk = wl * k_f[gs]
            qcs, qns = [], []
            for bb in range(GROUP_ROWS // P_T):
                b = g * (GROUP_ROWS // P_T) + bb
                rb = slice(g * GROUP_ROWS + bb * P_T, g * GROUP_ROWS + (bb + 1) * P_T)
                rl = slice(bb * P_T, (bb + 1) * P_T)
                c_old = c_ref[0, b, h]
                n_old = n_ref[b:b + 1, hs]
                qcs.append(_mm(q_bf[rb], c_old.astype(BF16)))
                qns.append(jnp.sum(q_f[rb] * n_old, axis=-1, keepdims=True))
                upd = lax.dot_general(k_bf[rb], wlv[rl], (((0,), (0,)), ((), ())), preferred_element_type=F32)
                dlb = dl[rl]
                c_ref[0, b, h] = jnp.concatenate([dlb, dlb], axis=0) * c_old + upd
                n_ref[b:b + 1, hs] = dlb[0:1] * n_old + jnp.sum(wlk[rl], axis=0, keepdims=True)
            num = num + decay * jnp.concatenate(qcs, axis=0)
            qn = rs + decay * jnp.concatenate(qns, axis=0)
            den = jnp.maximum(jnp.abs(qn), colt[gs, COL_EM + h:COL_EM + h + 1])
            ha_s[gs, hs] = num / den

    ha = ha_s[...] * jax.nn.sigmoid(_mm(a, wbig_ref[:, 1024:1536]))
    ha = jnp.concatenate([_head_norm(ha[:, h * HEAD_DIM:(h + 1) * HEAD_DIM]) for h in range(N_HEADS)], axis=-1)
    ha = (ha * lnhead_ref[...] + skip_ref[...] * xc) * _silu(_mm(a, wbig_ref[:, 512:1024]))

    u = _mm(a, wbig_ref[:, 1536:2048])
    for b in range(P_NB):
        for j in range(P_T // 8):
            for lc in range(MIX // 128):
                utm_s[lc, pl.ds(j * 64 + b, 8, stride=8), :] = (
                    u[b * P_T + j * 8:b * P_T + j * 8 + 8, lc * 128:(lc + 1) * 128])
    _s5_in(jnp.concatenate([utm_s[lc] for lc in range(MIX // 128)], axis=-1), wb_ref, xsre_s, xsim_s)
    _s5_scan(xsre_s, xsim_s, are_ref, aim_ref, sre_ref, sim_ref, 0, slice(0, 8), P_T)
    for hf in range(2):
        y_half = _s5_out(xsre_s, xsim_s, wcre_ref, wcim_ref, hf)
        ytm_s[2 * hf] = y_half[:, 0:128]
        ytm_s[2 * hf + 1] = y_half[:, 128:256]
    for b in range(P_NB):
        for j in range(P_T // 8):
            for lc in range(MIX // 128):
                ybm_s[b * P_T + j * 8:b * P_T + j * 8 + 8, lc * 128:(lc + 1) * 128] = (
                    ytm_s[lc, pl.ds(j * 64 + b, 8, stride=8), :])
    yb = _s5_post(ybm_s[...], u, a, wbig_ref, wglu_ref, s5d_ref, bglu_ref)

    p = p_ref[...].reshape(ROWS, D_PLE)
    y = _merge_out(x, p, ha, yb, wout_ref, wple_ref, wpg_ref, lnple_ref, lnfin_ref)
    y_ref[...] = y.reshape(P_NB, P_T, D_MODEL)


def _const_spec(shape):
    nd = len(shape)
    return pl.BlockSpec(shape, lambda i, nd=nd: (0,) * nd, pipeline_mode=pl.Buffered(1))


def _prompt_call(x, p, weights, vecs, are8, aim8):
    nb, seq, _ = x.shape
    assert nb == P_NB and seq % P_T == 0
    n_steps = seq // P_T
    w_specs = [_const_spec(w.shape) for w in weights]
    v_specs = [_const_spec(v.shape) for v in vecs]
    in_specs = ([pl.BlockSpec((P_NB, P_T, D_MODEL), lambda i: (0, i, 0)),
                 pl.BlockSpec((P_NB, P_T, D_PLE), lambda i: (0, i, 0))]
                + w_specs + v_specs + [_const_spec((8, N_STATE)), _const_spec((8, N_STATE))])
    out_shape = (
        jax.ShapeDtypeStruct((nb, seq, D_MODEL), F32),
        jax.ShapeDtypeStruct((1, nb, N_HEADS, HEAD_DIM, HEAD_DIM), F32),
        jax.ShapeDtypeStruct((nb, MIX), F32),
        jax.ShapeDtypeStruct((8, ROWS), F32),
        jax.ShapeDtypeStruct((1, nb, CONV_W - 1, MIX), F32),
        jax.ShapeDtypeStruct((8, N_STATE), F32),
        jax.ShapeDtypeStruct((8, N_STATE), F32),
    )
    out_specs = (
        pl.BlockSpec((P_NB, P_T, D_MODEL), lambda i: (0, i, 0)),
        pl.BlockSpec((1, nb, N_HEADS, HEAD_DIM, HEAD_DIM), lambda i: (0, 0, 0, 0, 0)),
        pl.BlockSpec((nb, MIX), lambda i: (0, 0)),
        pl.BlockSpec((8, ROWS), lambda i: (0, 0)),
        pl.BlockSpec((1, nb, CONV_W - 1, MIX), lambda i: (0, 0, 0, 0)),
        pl.BlockSpec((8, N_STATE), lambda i: (0, 0)),
        pl.BlockSpec((8, N_STATE), lambda i: (0, 0)),
    )
    scratch = [
        pltpu.VMEM((P_NB, P_T + 8, MIX), F32),
        pltpu.VMEM((ROWS, MIX), F32),
        pltpu.VMEM((MIX // 128, ROWS, 128), F32),
        pltpu.VMEM((MIX // 128, ROWS, 128), F32),
        pltpu.VMEM((ROWS, MIX), F32),
        pltpu.VMEM((ROWS, N_STATE), F32),
        pltpu.VMEM((ROWS, N_STATE), F32),
    ]
    return pl.pallas_call(
        _prompt_body,
        grid=(n_steps,),
        in_specs=in_specs,
        out_specs=out_specs,
        out_shape=out_shape,
        scratch_shapes=scratch,
        compiler_params=pltpu.CompilerParams(dimension_semantics=("arbitrary",), vmem_limit_bytes=VMEM_LIMIT),
        name="prompt_layer",
    )(x, p, *weights, *vecs, are8, aim8)


S_T = 4
S_RB = ROWS // (8 * S_T)
S_BLK = 8 * S_T


def _sample_front_body(x_ref, wbig_ref, wgate_ref, wqk_ref, wv_ref, wb_ref, wcre_ref, wcim_ref, wglu_ref,
                       lnmix_ref, convw_ref, convb_ref, bi8_ref, bf8_ref, s5d_ref, bglu_ref, are_ref, aim_ref,
                       n_in_ref, mrow_in_ref, conv_in_ref, sre_in_ref, sim_in_ref,
                       q_ref, k_ref, wlv_ref, colt_ref, num_ref, dec_ref, den_ref, og_ref, zg_ref, xc_ref, yb_ref,
                       n_ref, mrow_ref, conv_ref, sre_ref, sim_ref,
                       ext_s, xsre_s, xsim_s):
    x = x_ref[...]
    a = _rms(x, lnmix_ref[...]).astype(BF16)

    xm = _mm(a, wbig_ref[:, 0:512])
    ext_rows = (CONV_W - 1 + S_T) * 8
    ycs = []
    for rb in range(S_RB):
        ext_s[rb * ext_rows:rb * ext_rows + 24, :] = conv_in_ref[rb * 24:(rb + 1) * 24, :]
        ext_s[rb * ext_rows + 24:(rb + 1) * ext_rows, :] = xm[rb * S_BLK:(rb + 1) * S_BLK, :]
        yc = convb_ref[...]
        for j in range(CONV_W):
            yc = yc + ext_s[rb * ext_rows + 8 * j:rb * ext_rows + 8 * j + S_BLK, :] * convw_ref[j:j + 1, :]
        ycs.append(yc)
        conv_ref[rb * 24:(rb + 1) * 24, :] = ext_s[rb * ext_rows + S_BLK:(rb + 1) * ext_rows, :]
    xc = _silu(jnp.concatenate(ycs, axis=0))
    xc_ref[...] = xc
    xc_bf = xc.astype(BF16)
    xm_bf = xm.astype(BF16)

    mprev = mrow_in_ref[...]
    c8, colt, mnew = _gate_rows(_mm(a, wgate_ref[...]), bi8_ref[...], bf8_ref[...], mprev, S_T, 8)
    mrow_ref[...] = mnew
    colt_ref[...] = colt

    r = lax.broadcasted_iota(jnp.int32, (GROUP_ROWS, GROUP_ROWS), 0)
    cc = lax.broadcasted_iota(jnp.int32, (GROUP_ROWS, GROUP_ROWS), 1)
    same_seq = ((r // S_BLK) == (cc // S_BLK)) & ((r % 8) == (cc % 8))
    mask = same_seq & (((cc // 8) % S_T) <= ((r // 8) % S_T))

    nb = S_RB * 8
    for h in range(N_HEADS):
        hs = slice(h * HEAD_DIM, (h + 1) * HEAD_DIM)
        qk = _mm(xc_bf[:, hs], wqk_ref[h])
        q_f = qk[:, :HEAD_DIM]
        k_f = qk[:, HEAD_DIM:] * K_SCALE
        v_f = _mm(xm_bf[:, hs], wv_ref[h])
        q_bf = q_f.astype(BF16)
        k_bf = k_f.astype(BF16)
        v_bf = v_f.astype(BF16)
        decay = colt[:, COL_DECAY + h:COL_DECAY + h + 1]
        wl = colt[:, COL_WL + h:COL_WL + h + 1]
        dl = colt[:, COL_DL + h:COL_DL + h + 1]
        nums, rss = [], []
        for g in range(ROWS // GROUP_ROWS):
            gs = slice(g * GROUP_ROWS, (g + 1) * GROUP_ROWS)
            num, rs = _intra(q_bf[gs], k_bf[gs], v_bf[gs], colt[gs, COL_NEGM + h:COL_NEGM + h + 1],
                             c8[h:h + 1, gs], mask)
            nums.append(num)
            rss.append(rs)
        q_ref[:, hs] = q_bf
        k_ref[:, hs] = k_bf
        wlv_ref[:, hs] = (wl * v_f).astype(BF16)
        num_ref[:, hs] = jnp.concatenate(nums, axis=0)
        n_old = n_in_ref[:, hs]
        n_rows = jnp.broadcast_to(n_old.reshape(S_RB, 1, 8, HEAD_DIM), (S_RB, S_T, 8, HEAD_DIM)).reshape(ROWS, HEAD_DIM)
        qn = jnp.concatenate(rss, axis=0) + decay * jnp.sum(q_f * n_rows, axis=-1, keepdims=True)
        den = jnp.maximum(jnp.abs(qn), colt[:, COL_EM + h:COL_EM + h + 1])
        dec_ref[:, hs] = jnp.broadcast_to(decay, (ROWS, HEAD_DIM))
        den_ref[:, hs] = jnp.broadcast_to(den, (ROWS, HEAD_DIM))
        wlk_sum = jnp.sum((wl * k_f).reshape(S_RB, S_T, 8, HEAD_DIM), axis=1).reshape(nb, HEAD_DIM)
        dl_seq = dl.reshape(S_RB, S_T, 8, 1)[:, 0].reshape(nb, 1)
        n_ref[:, hs] = dl_seq * n_old + wlk_sum

    og_ref[...] = jax.nn.sigmoid(_mm(a, wbig_ref[:, 1024:1536]))
    zg_ref[...] = _silu(_mm(a, wbig_ref[:, 512:1024]))

    u = _mm(a, wbig_ref[:, 1536:2048])
    _s5_in(u, wb_ref, xsre_s, xsim_s)
    sre_ref[...] = sre_in_ref[...]
    sim_ref[...] = sim_in_ref[...]
    for rb in range(S_RB):
        _s5_scan(xsre_s, xsim_s, are_ref, aim_ref, sre_ref, sim_ref, rb * S_BLK, slice(rb * 8, (rb + 1) * 8), S_T)
    y_s5 = jnp.concatenate([_s5_out(xsre_s, xsim_s, wcre_ref, wcim_ref, hf) for hf in range(2)], axis=-1)
    yb_ref[...] = _s5_post(y_s5, u, a, wbig_ref, wglu_ref, s5d_ref, bglu_ref)


def _sample_front_call(x_tm, weights, vecs, are8, aim8, n2, mrow_in, conv_tm, sre2, sim2):
    (wbig, wgate, wqk, wv, wb, wcre, wcim, wglu, _, _, _) = weights
    (lnmix, convw, convb, bi8, bf8, _, _, s5d, bglu, _, _) = vecs
    nb = S_RB * 8
    f32 = lambda *s: jax.ShapeDtypeStruct(s, F32)
    bf = lambda *s: jax.ShapeDtypeStruct(s, BF16)
    out_shape = (bf(ROWS, MIX), bf(ROWS, MIX), bf(ROWS, MIX), f32(ROWS, 128), f32(ROWS, MIX), f32(ROWS, MIX),
                 f32(ROWS, MIX), f32(ROWS, MIX), f32(ROWS, MIX), f32(ROWS, MIX), f32(ROWS, MIX),
                 f32(nb, MIX), f32(8, ROWS), f32(nb * (CONV_W - 1), MIX), f32(nb, N_STATE), f32(nb, N_STATE))
    scratch = [pltpu.VMEM((S_RB * (CONV_W - 1 + S_T) * 8, MIX), F32),
               pltpu.VMEM((ROWS, N_STATE), F32), pltpu.VMEM((ROWS, N_STATE), F32)]
    return pl.pallas_call(
        _sample_front_body, out_shape=out_shape, scratch_shapes=scratch,
        compiler_params=pltpu.CompilerParams(vmem_limit_bytes=VMEM_LIMIT),
        name="sample_front",
    )(x_tm, wbig, wgate, wqk, wv, wb, wcre, wcim, wglu, lnmix, convw, convb, bi8, bf8, s5d, bglu, are8, aim8,
      n2, mrow_in, conv_tm, sre2, sim2)


def _sample_state_body(c_ref, q_ref, k_ref, wlv_ref, colt_ref, qc_ref, cnew_ref):
    sub = lax.broadcasted_iota(jnp.int32, (S_BLK, HEAD_DIM), 0) % 8
    for h in range(N_HEADS):
        hs = slice(h * HEAD_DIM, (h + 1) * HEAD_DIM)
        q = q_ref[:, hs]
        k = k_ref[:, hs]
        wlv = wlv_ref[:, hs]
        acc = jnp.zeros((S_BLK, HEAD_DIM), F32)
        for b8 in range(8):
            c_old = c_ref[0, b8, h]
            sel = sub == b8
            acc = jnp.where(sel, _mm(q, c_old.astype(BF16)), acc)
            upd = lax.dot_general(k, jnp.where(sel, wlv, jnp.zeros_like(wlv)), (((0,), (0,)), ((), ())),
                                  preferred_element_type=F32)
            dl = colt_ref[b8:b8 + 1, COL_DL + h:COL_DL + h + 1]
            cnew_ref[0, b8, h] = dl * c_old + upd
        qc_ref[:, hs] = acc


def _sample_state_call(c_state, q_bf, k_bf, wlv_bf, colt):
    row_spec = lambda w: pl.BlockSpec((S_BLK, w), lambda i: (i, 0))
    c_spec = pl.BlockSpec((1, 8, N_HEADS, HEAD_DIM, HEAD_DIM), lambda i: (0, i, 0, 0, 0))
    return pl.pallas_call(
        _sample_state_body,
        grid=(S_RB,),
        in_specs=[c_spec, row_spec(MIX), row_spec(MIX), row_spec(MIX), row_spec(128)],
        out_specs=(row_spec(MIX), c_spec),
        out_shape=(jax.ShapeDtypeStruct((ROWS, MIX), F32), jax.ShapeDtypeStruct(c_state.shape, F32)),
        compiler_params=pltpu.CompilerParams(dimension_semantics=("arbitrary",), vmem_limit_bytes=VMEM_LIMIT),
        name="sample_state",
    )(c_state, q_bf, k_bf, wlv_bf, colt)


def _sample_back_body(x_ref, p_ref, num_ref, dec_ref, den_ref, qc_ref, og_ref, zg_ref, xc_ref, yb_ref,
                      wout_ref, wple_ref, wpg_ref, lnhead_ref, skip_ref, lnple_ref, lnfin_ref, y_ref):
    ha = (num_ref[...] + dec_ref[...] * qc_ref[...]) / den_ref[...] * og_ref[...]
    ha = jnp.concatenate([_head_norm(ha[:, h * HEAD_DIM:(h + 1) * HEAD_DIM]) for h in range(N_HEADS)], axis=-1)
    ha = (ha * lnhead_ref[...] + skip_ref[...] * xc_ref[...]) * zg_ref[...]
    y_ref[...] = _merge_out(x_ref[...], p_ref[...], ha, yb_ref[...], wout_ref, wple_ref, wpg_ref,
                            lnple_ref, lnfin_ref)


def _sample_back_call(x_tm, p_tm, num, dec, den, qc, og, zg, xc, yb, weights, vecs):
    (_, _, _, _, _, _, _, _, wout, wple, wpg) = weights
    (_, _, _, _, _, lnhead, skip, _, _, lnple, lnfin) = vecs
    return pl.pallas_call(
        _sample_back_body, out_shape=jax.ShapeDtypeStruct((ROWS, D_MODEL), F32),
        compiler_params=pltpu.CompilerParams(vmem_limit_bytes=VMEM_LIMIT),
        name="sample_back",
    )(x_tm, p_tm, num, dec, den, qc, og, zg, xc, yb, wout, wple, wpg, lnhead, skip, lnple, lnfin)


def _to_blocks(a, n_t):
    nb, _, w = a.shape
    return jnp.transpose(a.reshape(nb // 8, 8, n_t, w), (0, 2, 1, 3)).reshape(nb * n_t, w)


def _from_blocks(a, n_t):
    rows, w = a.shape
    nb = rows // n_t
    return jnp.transpose(a.reshape(nb // 8, n_t, 8, w), (0, 2, 1, 3)).reshape(nb, n_t, w)


def _sample_path(x_sample, p_sample, c_state, n_state, m_state, conv_state, sre_state, sim_state,
                 weights, vecs, are8, aim8):
    nb, n_t, _ = x_sample.shape
    assert nb == S_RB * 8 and n_t == S_T
    x_tm = _to_blocks(x_sample, S_T)
    p_tm = _to_blocks(p_sample, S_T)
    conv_tm = _to_blocks(conv_state, CONV_W - 1)
    m_rows = jnp.broadcast_to(jnp.transpose(m_state).reshape(N_HEADS, S_RB, 1, 8), (N_HEADS, S_RB, S_T, 8))
    mrow_in = jnp.zeros((8, ROWS), F32).at[0:N_HEADS].set(m_rows.reshape(N_HEADS, ROWS))
    (q_bf, k_bf, wlv_bf, colt, num, dec, den, og, zg, xc, yb, n_new, mrow, conv_new, sre, sim) = _sample_front_call(
        x_tm, weights, vecs, are8, aim8, n_state.reshape(nb, MIX), mrow_in, conv_tm,
        sre_state.reshape(nb, N_STATE), sim_state.reshape(nb, N_STATE))
    qc, c_new = _sample_state_call(c_state, q_bf, k_bf, wlv_bf, colt)
    y_tm = _sample_back_call(x_tm, p_tm, num, dec, den, qc, og, zg, xc, yb, weights, vecs)
    m_new = jnp.transpose(mrow[0:N_HEADS].reshape(N_HEADS, S_RB, S_T, 8)[:, :, 0, :].reshape(N_HEADS, nb))
    return (_from_blocks(y_tm, S_T), c_new, n_new.reshape(1, nb, N_HEADS, HEAD_DIM), m_new[None],
            _from_blocks(conv_new, CONV_W - 1)[None], sre.reshape(1, nb, N_GROUPS, S5_STATE),
            sim.reshape(1, nb, N_GROUPS, S5_STATE))


def _layout_params(ln_mix, w_in, b_igate, b_fgate, conv_w, conv_b, w_q, w_k, w_v, ln_head, skip_a,
                   s5_B_re, s5_B_im, s5_C_re, s5_C_im, s5_D, w_glu, b_glu, w_out, w_ple, ln_ple,
                   w_ple_gate, ln_final):
    w = w_in[0]
    o_i = 3 * MIX
    o_f = o_i + N_HEADS
    o_s = o_f + N_HEADS
    wbig = jnp.concatenate([w[:, 0:3 * MIX], w[:, o_s:o_s + 2 * MIX]], axis=1).astype(BF16)
    wgate = jnp.zeros((D_MODEL, 128), F32)
    wgate = wgate.at[:, 0:N_HEADS].set(w[:, o_i:o_f]).at[:, 8:8 + N_HEADS].set(w[:, o_f:o_s]).astype(BF16)
    wqk = jnp.concatenate([w_q[0], w_k[0]], axis=-1).astype(BF16)
    wv = w_v[0].astype(BF16)
    eye = jnp.eye(16, dtype=F32)

    def c_blocks(cmat):
        cr = cmat.reshape(2, 16, S5_GROUP, S5_STATE)
        return jnp.einsum('ab,hbcp->hapbc', eye, cr).reshape(2, 16 * S5_STATE, 16 * S5_GROUP).astype(BF16)

    def b_tiled(bmat):
        bt = jnp.transpose(bmat.reshape(2, 16, S5_STATE, S5_GROUP), (0, 1, 3, 2)).reshape(2, 256, 1, S5_STATE)
        return jnp.broadcast_to(bt, (2, 256, 16, S5_STATE)).reshape(2, 256, 16 * S5_STATE)

    weights = [wbig, wgate, wqk, wv, None, c_blocks(s5_C_re[0]), c_blocks(s5_C_im[0]),
               w_glu[0].astype(BF16), w_out[0].astype(BF16), w_ple[0].astype(BF16), w_ple_gate[0].astype(BF16)]
    pad4 = jnp.zeros((8, 1), F32)
    vecs = [ln_mix[0][None, :], conv_w[0], conv_b[0][None, :],
            pad4.at[0:N_HEADS, 0].set(b_igate[0]), pad4.at[0:N_HEADS, 0].set(b_fgate[0]),
            ln_head[0][None, :], skip_a[0][None, :], s5_D[0][None, :], b_glu[0][None, :],
            ln_ple[0][None, :], ln_final[None, :]]
    return weights, vecs, b_tiled(s5_B_re[0]), b_tiled(s5_B_im[0])


def kernel(x_prompt, x_sample, p_prompt, p_sample, state_mlstm_C, state_mlstm_n, state_mlstm_m, state_conv, state_s5_re, state_s5_im, ln_mix, w_in, b_igate, b_fgate, conv_w, conv_b, w_q, w_k, w_v, ln_head, skip_a, s5_lam_re, s5_lam_im, s5_log_dt, s5_B_re, s5_B_im, s5_C_re, s5_C_im, s5_D, w_glu, b_glu, w_out, w_ple, ln_ple, w_ple_gate, ln_final):
    weights, vecs, bt_re, bt_im = _layout_params(
        ln_mix, w_in, b_igate, b_fgate, conv_w, conv_b, w_q, w_k, w_v, ln_head, skip_a,
        s5_B_re, s5_B_im, s5_C_re, s5_C_im, s5_D, w_glu, b_glu, w_out, w_ple, ln_ple, w_ple_gate, ln_final)
    are8, aim8, wb = _prep_call(s5_lam_re.reshape(1, N_STATE), s5_lam_im.reshape(1, N_STATE),
                                s5_log_dt.reshape(1, N_STATE), bt_re, bt_im)
    weights[4] = wb

    nbp = x_prompt.shape[0]
    y_p, c_p, n_p, mrow_p, conv_p, sre_p, sim_p = _prompt_call(x_prompt, p_prompt[0], weights, vecs, are8, aim8)
    pn = n_p.reshape(1, nbp, N_HEADS, HEAD_DIM)
    pm = jnp.transpose(mrow_p[0:N_HEADS, ::P_T])[None]
    pre = sre_p.reshape(1, nbp, N_GROUPS, S5_STATE)
    pim = sim_p.reshape(1, nbp, N_GROUPS, S5_STATE)

    y_s, c_s, n_s, m_s, conv_s, sre_s, sim_s = _sample_path(
        x_sample, p_sample[0], state_mlstm_C, state_mlstm_n[0], state_mlstm_m[0], state_conv[0],
        state_s5_re[0], state_s5_im[0], weights, vecs, are8, aim8)
    return (y_p, y_s, c_p, pn, pm, conv_p, pre, pim, c_s, n_s, m_s, conv_s, sre_s, sim_s)
```

```python
import functools
import math

import jax
import jax.numpy as jnp
from jax import lax
from jax.experimental import pallas as pl
from jax.experimental.pallas import tpu as pltpu

F32 = jnp.float32
BF16 = jnp.bfloat16

D_MODEL = 1024
D_PLE = 256
MIX = 512
N_HEADS = 4
HEAD_DIM = 128
N_GROUPS = 32
S5_GROUP = 16
S5_STATE = 64
N_STATE = N_GROUPS * S5_STATE
CONV_W = 4
EPS = 1e-6
K_SCALE = HEAD_DIM ** -0.5
ROWS = 512
GROUP_ROWS = 256
VMEM_LIMIT = 56 * 1024 * 1024


def _mm(a, w):
    return jnp.dot(a, w, preferred_element_type=F32)


def _rms(x, g):
    ms = jnp.mean(x * x, axis=-1, keepdims=True)
    return x * lax.rsqrt(ms + EPS) * g


def _silu(x):
    return x * jax.nn.sigmoid(x)


def _gelu_tanh(x):
    c = math.sqrt(2.0 / math.pi)
    return 0.5 * x * (1.0 + jnp.tanh(c * (x + 0.044715 * (x * x * x))))


def _log_sigmoid(x):
    return jnp.minimum(x, 0.0) - jnp.log1p(jnp.exp(-jnp.abs(x)))


SCAN_RADIX = 8


def _seg_scan(x, tidx, n_t, t_stride, op, fill):
    span = 1
    while span < n_t:
        acc = x
        for j in range(1, SCAN_RADIX):
            if j * span >= n_t:
                break
            prev = pltpu.roll(x, j * span * t_stride, axis=1)
            acc = op(acc, jnp.where(tidx >= j * span, prev, fill))
        x = acc
        span *= SCAN_RADIX
    return x


def _bcast_last(x, tidx, n_t, t_stride):
    y = jnp.where(tidx == n_t - 1, x, 0.0)
    span = 1
    while span < n_t:
        acc = y
        for j in range(1, SCAN_RADIX):
            if j * span >= n_t:
                break
            nxt = pltpu.roll(y, ROWS - j * span * t_stride, axis=1)
            acc = acc + jnp.where(tidx + j * span <= n_t - 1, nxt, 0.0)
        y = acc
        span *= SCAN_RADIX
    return y


def _gate_rows(g_cols, bi8, bf8, mprev, n_t, t_stride):
    gt = g_cols.T
    lane = lax.broadcasted_iota(jnp.int32, (8, ROWS), 1)
    tidx = (lane // t_stride) % n_t
    i8 = gt[0:8] + bi8
    logf = _log_sigmoid(gt[8:16] + bf8)
    b = _seg_scan(logf, tidx, n_t, t_stride, jnp.add, 0.0)
    c = i8 - b
    cm = _seg_scan(c, tidx, n_t, t_stride, jnp.maximum, -jnp.inf)
    big_m = jnp.maximum(mprev, cm)
    mt = b + big_m
    m_last = _bcast_last(big_m, tidx, n_t, t_stride)
    mnew = _bcast_last(mt, tidx, n_t, t_stride)
    stack = jnp.concatenate(
        [-big_m, jnp.exp(mprev - big_m), jnp.exp(-mt), jnp.exp(c - m_last), jnp.exp(mprev - m_last),
         jnp.zeros((128 - 40, ROWS), F32)], axis=0)
    return c, stack.T, mnew


COL_NEGM, COL_DECAY, COL_EM, COL_WL, COL_DL = 0, 8, 16, 24, 32


def _intra(qg, kg, vg, col_a, row_b, mask):
    s = lax.dot_general(qg, kg, (((1,), (1,)), ((), ())), preferred_element_type=F32)
    p = s * jnp.exp(jnp.where(mask, col_a + row_b, -jnp.inf))
    rs = jnp.sum(p, axis=-1, keepdims=True)
    return _mm(p.astype(BF16), vg), rs


def _head_norm(x):
    mu = jnp.mean(x, axis=-1, keepdims=True)
    xc = x - mu
    var = jnp.mean(xc * xc, axis=-1, keepdims=True)
    return xc * lax.rsqrt(var + EPS)


def _s5_scan(xsre_s, xsim_s, are_ref, aim_ref, sre_ref, sim_ref, row_base, st_rows, n_t):
    chunk = 512
    for ch in range(N_STATE // chunk):
        ls = slice(ch * chunk, (ch + 1) * chunk)
        ar = are_ref[:, ls]
        ai = aim_ref[:, ls]

        xr = sre_ref[st_rows, ls]
        xi = sim_ref[st_rows, ls]
        for t in range(n_t):
            rs = slice(row_base + t * 8, row_base + t * 8 + 8)
            xr, xi = (ar * xr - ai * xi + xsre_s[rs, ls], ar * xi + ai * xr + xsim_s[rs, ls])
            xsre_s[rs, ls] = xr
            xsim_s[rs, ls] = xi
        sre_ref[st_rows, ls] = xr
        sim_ref[st_rows, ls] = xi


def _s5_in(u_tm, wb_ref, xsre_s, xsim_s):
    half = N_STATE // 2
    for hf in range(2):
        res = _mm(u_tm[:, hf * 256:(hf + 1) * 256].astype(BF16), wb_ref[hf])
        xsre_s[:, hf * half:(hf + 1) * half] = res[:, :half]
        xsim_s[:, hf * half:(hf + 1) * half] = res[:, half:]


def _s5_out(xsre_s, xsim_s, wcre_ref, wcim_ref, hf):
    half = N_STATE // 2
    hs = slice(hf * half, (hf + 1) * half)
    return (_mm(xsre_s[:, hs].astype(BF16), wcre_ref[hf])
            - _mm(xsim_s[:, hs].astype(BF16), wcim_ref[hf]))


def _s5_post(y_s5, u, a, wbig_ref, wglu_ref, s5d_ref, bglu_ref):
    yb = _gelu_tanh(y_s5 + s5d_ref[...] * u)
    yb = yb * jax.nn.sigmoid(_mm(yb.astype(BF16), wglu_ref[...]) + bglu_ref[...])
    return yb * _silu(_mm(a, wbig_ref[:, 2048:2560]))


def _merge_out(x, p, h_a, y_b, wout_ref, wple_ref, wpg_ref, lnple_ref, lnfin_ref):
    h1 = x + _mm(h_a.astype(BF16), wout_ref[0:MIX, :]) + _mm(y_b.astype(BF16), wout_ref[MIX:2 * MIX, :])
    e = _rms(_mm(p.astype(BF16), wple_ref[...]), lnple_ref[...])
    h2 = h1 + jax.nn.sigmoid(_mm(h1.astype(BF16), wpg_ref[...])) * e
    return _rms(h2, lnfin_ref[...])


def _prep_body(lr_ref, li_ref, ldt_ref, btr_ref, bti_ref, are_ref, aim_ref, wb_ref):
    lr = jnp.minimum(lr_ref[...], -1e-4)
    li = li_ref[...]
    dt = jnp.exp(ldt_ref[...])
    mag = jnp.exp(lr * dt)
    a_re = mag * jnp.cos(li * dt)
    a_im = mag * jnp.sin(li * dt)
    den = lr * lr + li * li
    xr = a_re - 1.0
    g_re = (xr * lr + a_im * li) / den
    g_im = (a_im * lr - xr * li) / den
    are_ref[...] = jnp.broadcast_to(a_re, (8, N_STATE))
    aim_ref[...] = jnp.broadcast_to(a_im, (8, N_STATE))
    half = N_STATE // 2
    row = lax.broadcasted_iota(jnp.int32, (256, half), 0)
    col = lax.broadcasted_iota(jnp.int32, (256, half), 1)
    mask = (row // S5_GROUP) == (col // S5_STATE)
    for hf in range(2):
        gr = g_re[:, hf * half:(hf + 1) * half]
        gi = g_im[:, hf * half:(hf + 1) * half]
        br = btr_ref[hf]
        bi = bti_ref[hf]
        wb_ref[hf, :, 0:half] = jnp.where(mask, gr * br - gi * bi, 0.0).astype(BF16)
        wb_ref[hf, :, half:2 * half] = jnp.where(mask, gr * bi + gi * br, 0.0).astype(BF16)


def _prep_call(lam_re, lam_im, log_dt, bt_re, bt_im):
    return pl.pallas_call(
        _prep_body,
        out_shape=(jax.ShapeDtypeStruct((8, N_STATE), F32), jax.ShapeDtypeStruct((8, N_STATE), F32),
                   jax.ShapeDtypeStruct((2, 256, N_STATE), BF16)),
        name="s5_prep",
    )(lam_re, lam_im, log_dt, bt_re, bt_im)


P_T = 64
P_NB = 8


def _prompt_body(x_ref, p_ref, wbig_ref, wgate_ref, wqk_ref, wv_ref, wb_ref, wcre_ref, wcim_ref,
                 wglu_ref, wout_ref, wple_ref, wpg_ref, lnmix_ref, convw_ref, convb_ref, bi8_ref, bf8_ref,
                 lnhead_ref, skip_ref, s5d_ref, bglu_ref, lnple_ref, lnfin_ref, are_ref, aim_ref,
                 y_ref, c_ref, n_ref, mrow_ref, conv_ref, sre_ref, sim_ref,
                 ext_s, ha_s, utm_s, ytm_s, ybm_s, xsre_s, xsim_s):
    step = pl.program_id(0)

    @pl.when(step == 0)
    def _init():
        c_ref[...] = jnp.zeros_like(c_ref)
        n_ref[...] = jnp.zeros_like(n_ref)
        mrow_ref[...] = jnp.zeros_like(mrow_ref)
        sre_ref[...] = jnp.zeros_like(sre_ref)
        sim_ref[...] = jnp.zeros_like(sim_ref)
        ext_s[:, 0:8, :] = jnp.zeros((P_NB, 8, MIX), F32)

    x = x_ref[...].reshape(ROWS, D_MODEL)
    a = _rms(x, lnmix_ref[...]).astype(BF16)

    c8, colt, mnew = _gate_rows(_mm(a, wgate_ref[...]), bi8_ref[...], bf8_ref[...], mrow_ref[...], P_T, 1)
    mrow_ref[...] = mnew

    xm = _mm(a, wbig_ref[:, 0:512])
    ext_s[:, 8:8 + P_T, :] = xm.reshape(P_NB, P_T, MIX)
    yc = convb_ref[...]
    for j in range(CONV_W):
        yc = yc + ext_s[:, 5 + j:5 + j + P_T, :].reshape(ROWS, MIX) * convw_ref[j:j + 1, :]
    tail = ext_s[:, 5 + P_T:8 + P_T, :]
    conv_ref[0] = tail
    ext_s[:, 5:8, :] = tail
    xc = _silu(yc)
    xc_bf = xc.astype(BF16)
    xm_bf = xm.astype(BF16)

    r = lax.broadcasted_iota(jnp.int32, (GROUP_ROWS, GROUP_ROWS), 0)
    cc = lax.broadcasted_iota(jnp.int32, (GROUP_ROWS, GROUP_ROWS), 1)
    mask = ((r // P_T) == (cc // P_T)) & (cc <= r)

    for h in range(N_HEADS):
        hs = slice(h * HEAD_DIM, (h + 1) * HEAD_DIM)
        qk = _mm(xc_bf[:, hs], wqk_ref[h])
        q_f = qk[:, :HEAD_DIM]
        k_f = qk[:, HEAD_DIM:] * K_SCALE
        v_f = _mm(xm_bf[:, hs], wv_ref[h])
        q_bf = q_f.astype(BF16)
        k_bf = k_f.astype(BF16)
        v_bf = v_f.astype(BF16)
        for g in range(ROWS // GROUP_ROWS):
            gs = slice(g * GROUP_ROWS, (g + 1) * GROUP_ROWS)
            decay = colt[gs, COL_DECAY + h:COL_DECAY + h + 1]
            wl = colt[gs, COL_WL + h:COL_WL + h + 1]
            dl = colt[gs, COL_DL + h:COL_DL + h + 1]
            num, rs = _intra(q_bf[gs], k_bf[gs], v_bf[gs], colt[gs, COL_NEGM + h:COL_NEGM + h + 1],
                             c8[h:h + 1, gs], mask)
            wlv = (wl * v_f[gs]).astype(BF16)
            wlk = wl * k_f[gs]
            qcs, qns = [], []
            for bb in range(GROUP_ROWS // P_T):
                b = g * (GROUP_ROWS // P_T) + bb
                rb = slice(g * GROUP_ROWS + bb * P_T, g * GROUP_ROWS + (bb + 1) * P_T)
                rl = slice(bb * P_T, (bb + 1) * P_T)
                c_old = c_ref[0, b, h]
                n_old = n_ref[b:b + 1, hs]
                qcs.append(_mm(q_bf[rb], c_old.astype(BF16)))
                qns.append(jnp.sum(q_f[rb] * n_old, axis=-1, keepdims=True))
                upd = lax.dot_general(k_bf[rb], wlv[rl], (((0,), (0,)), ((), ())), preferred_element_type=F32)
                dlb = dl[rl]
                c_ref[0, b, h] = jnp.concatenate([dlb, dlb], axis=0) * c_old + upd
                n_ref[b:b + 1, hs] = dlb[0:1] * n_old + jnp.sum(wlk[rl], axis=0, keepdims=True)
            num = num + decay * jnp.concatenate(qcs, axis=0)
            qn = rs + decay * jnp.concatenate(qns, axis=0)
            den = jnp.maximum(jnp.abs(qn), colt[gs, COL_EM + h:COL_EM + h + 1])
            ha_s[gs, hs] = num / den

    ha = ha_s[...] * jax.nn.sigmoid(_mm(a, wbig_ref[:, 1024:1536]))
    ha = jnp.concatenate([_head_norm(ha[:, h * HEAD_DIM:(h + 1) * HEAD_DIM]) for h in range(N_HEADS)], axis=-1)
    ha = (ha * lnhead_ref[...] + skip_ref[...] * xc) * _silu(_mm(a, wbig_ref[:, 512:1024]))

    u = _mm(a, wbig_ref[:, 1536:2048])
    for b in range(P_NB):
        for j in range(P_T // 8):
            for lc in range(MIX // 128):
                utm_s[lc, pl.ds(j * 64 + b, 8, stride=8), :] = (
                    u[b * P_T + j * 8:b * P_T + j * 8 + 8, lc * 128:(lc + 1) * 128])
    _s5_in(jnp.concatenate([utm_s[lc] for lc in range(MIX // 128)], axis=-1), wb_ref, xsre_s, xsim_s)
    _s5_scan(xsre_s, xsim_s, are_ref, aim_ref, sre_ref, sim_ref, 0, slice(0, 8), P_T)
    for hf in range(2):
        y_half = _s5_out(xsre_s, xsim_s, wcre_ref, wcim_ref, hf)
        ytm_s[2 * hf] = y_half[:, 0:128]
        ytm_s[2 * hf + 1] = y_half[:, 128:256]
    for b in range(P_NB):
        for j in range(P_T // 8):
            for lc in range(MIX // 128):
                ybm_s[b * P_T + j * 8:b * P_T + j * 8 + 8, lc * 128:(lc + 1) * 128] = (
                    ytm_s[lc, pl.ds(j * 64 + b, 8, stride=8), :])
    yb = _s5_post(ybm_s[...], u, a, wbig_ref, wglu_ref, s5d_ref, bglu_ref)

    p = p_ref[...].reshape(ROWS, D_PLE)
    y = _merge_out(x, p, ha, yb, wout_ref, wple_ref, wpg_ref, lnple_ref, lnfin_ref)
    y_ref[...] = y.reshape(P_NB, P_T, D_MODEL)


def _const_spec(shape):
    nd = len(shape)
    return pl.BlockSpec(shape, lambda i, nd=nd: (0,) * nd, pipeline_mode=pl.Buffered(1))


def _prompt_call(x, p, weights, vecs, are8, aim8):
    nb, seq, _ = x.shape
    assert nb == P_NB and seq % P_T == 0
    n_steps = seq // P_T
    w_specs = [_const_spec(w.shape) for w in weights]
    v_specs = [_const_spec(v.shape) for v in vecs]
    in_specs = ([pl.BlockSpec((P_NB, P_T, D_MODEL), lambda i: (0, i, 0)),
                 pl.BlockSpec((P_NB, P_T, D_PLE), lambda i: (0, i, 0))]
                + w_specs + v_specs + [_const_spec((8, N_STATE)), _const_spec((8, N_STATE))])
    out_shape = (
        jax.ShapeDtypeStruct((nb, seq, D_MODEL), F32),
        jax.ShapeDtypeStruct((1, nb, N_HEADS, HEAD_DIM, HEAD_DIM), F32),
        jax.ShapeDtypeStruct((nb, MIX), F32),
        jax.ShapeDtypeStruct((8, ROWS), F32),
        jax.ShapeDtypeStruct((1, nb, CONV_W - 1, MIX), F32),
        jax.ShapeDtypeStruct((8, N_STATE), F32),
        jax.ShapeDtypeStruct((8, N_STATE), F32),
    )
    out_specs = (
        pl.BlockSpec((P_NB, P_T, D_MODEL), lambda i: (0, i, 0)),
        pl.BlockSpec((1, nb, N_HEADS, HEAD_DIM, HEAD_DIM), lambda i: (0, 0, 0, 0, 0)),
        pl.BlockSpec((nb, MIX), lambda i: (0, 0)),
        pl.BlockSpec((8, ROWS), lambda i: (0, 0)),
        pl.BlockSpec((1, nb, CONV_W - 1, MIX), lambda i: (0, 0, 0, 0)),
        pl.BlockSpec((8, N_STATE), lambda i: (0, 0)),
        pl.BlockSpec((8, N_STATE), lambda i: (0, 0)),
    )
    scratch = [
        pltpu.VMEM((P_NB, P_T + 8, MIX), F32),
        pltpu.VMEM((ROWS, MIX), F32),
        pltpu.VMEM((MIX // 128, ROWS, 128), F32),
        pltpu.VMEM((MIX // 128, ROWS, 128), F32),
        pltpu.VMEM((ROWS, MIX), F32),
        pltpu.VMEM((ROWS, N_STATE), F32),
        pltpu.VMEM((ROWS, N_STATE), F32),
    ]
    return pl.pallas_call(
        _prompt_body,
        grid=(n_steps,),
        in_specs=in_specs,
        out_specs=out_specs,
        out_shape=out_shape,
        scratch_shapes=scratch,
        compiler_params=pltpu.CompilerParams(dimension_semantics=("arbitrary",), vmem_limit_bytes=VMEM_LIMIT),
        name="prompt_layer",
    )(x, p, *weights, *vecs, are8, aim8)


S_T = 4
S_RB = ROWS // (8 * S_T)
S_BLK = 8 * S_T


def _sample_front_body(x_ref, wbig_ref, wgate_ref, wqk_ref, wv_ref, wb_ref, wcre_ref, wcim_ref, wglu_ref,
                       lnmix_ref, convw_ref, convb_ref, bi8_ref, bf8_ref, s5d_ref, bglu_ref, are_ref, aim_ref,
                       n_in_ref, mrow_in_ref, conv_in_ref, sre_in_ref, sim_in_ref,
                       q_ref, k_ref, wlv_ref, colt_ref, num_ref, dec_ref, den_ref, og_ref, zg_ref, xc_ref, yb_ref,
                       n_ref, mrow_ref, conv_ref, sre_ref, sim_ref,
                       ext_s, xsre_s, xsim_s):
    x = x_ref[...]
    a = _rms(x, lnmix_ref[...]).astype(BF16)

    xm = _mm(a, wbig_ref[:, 0:512])
    ext_rows = (CONV_W - 1 + S_T) * 8
    ycs = []
    for rb in range(S_RB):
        ext_s[rb * ext_rows:rb * ext_rows + 24, :] = conv_in_ref[rb * 24:(rb + 1) * 24, :]
        ext_s[rb * ext_rows + 24:(rb + 1) * ext_rows, :] = xm[rb * S_BLK:(rb + 1) * S_BLK, :]
        yc = convb_ref[...]
        for j in range(CONV_W):
            yc = yc + ext_s[rb * ext_rows + 8 * j:rb * ext_rows + 8 * j + S_BLK, :] * convw_ref[j:j + 1, :]
        ycs.append(yc)
        conv_ref[rb * 24:(rb + 1) * 24, :] = ext_s[rb * ext_rows + S_BLK:(rb + 1) * ext_rows, :]
    xc = _silu(jnp.concatenate(ycs, axis=0))
    xc_ref[...] = xc
    xc_bf = xc.astype(BF16)
    xm_bf = xm.astype(BF16)

    mprev = mrow_in_ref[...]
    c8, colt, mnew = _gate_rows(_mm(a, wgate_ref[...]), bi8_ref[...], bf8_ref[...], mprev, S_T, 8)
    mrow_ref[...] = mnew
    colt_ref[...] = colt

    r = lax.broadcasted_iota(jnp.int32, (GROUP_ROWS, GROUP_ROWS), 0)
    cc = lax.broadcasted_iota(jnp.int32, (GROUP_ROWS, GROUP_ROWS), 1)
    same_seq = ((r // S_BLK) == (cc // S_BLK)) & ((r % 8) == (cc % 8))
    mask = same_seq & (((cc // 8) % S_T) <= ((r // 8) % S_T))

    nb = S_RB * 8
    for h in range(N_HEADS):
        hs = slice(h * HEAD_DIM, (h + 1) * HEAD_DIM)
        qk = _mm(xc_bf[:, hs], wqk_ref[h])
        q_f = qk[:, :HEAD_DIM]
        k_f = qk[:, HEAD_DIM:] * K_SCALE
        v_f = _mm(xm_bf[:, hs], wv_ref[h])
        q_bf = q_f.astype(BF16)
        k_bf = k_f.astype(BF16)
        v_bf = v_f.astype(BF16)
        decay = colt[:, COL_DECAY + h:COL_DECAY + h + 1]
        wl = colt[:, COL_WL + h:COL_WL + h + 1]
        dl = colt[:, COL_DL + h:COL_DL + h + 1]
        nums, rss = [], []
        for g in range(ROWS // GROUP_ROWS):
            gs = slice(g * GROUP_ROWS, (g + 1) * GROUP_ROWS)
            num, rs = _intra(q_bf[gs], k_bf[gs], v_bf[gs], colt[gs, COL_NEGM + h:COL_NEGM + h + 1],
                             c8[h:h + 1, gs], mask)
            nums.append(num)
            rss.append(rs)
        q_ref[:, hs] = q_bf
        k_ref[:, hs] = k_bf
        wlv_ref[:, hs] = (wl * v_f).astype(BF16)
        num_ref[:, hs] = jnp.concatenate(nums, axis=0)
        n_old = n_in_ref[:, hs]
        n_rows = jnp.broadcast_to(n_old.reshape(S_RB, 1, 8, HEAD_DIM), (S_RB, S_T, 8, HEAD_DIM)).reshape(ROWS, HEAD_DIM)
        qn = jnp.concatenate(rss, axis=0) + decay * jnp.sum(q_f * n_rows, axis=-1, keepdims=True)
        den = jnp.maximum(jnp.abs(qn), colt[:, COL_EM + h:COL_EM + h + 1])
        dec_ref[:, hs] = jnp.broadcast_to(decay, (ROWS, HEAD_DIM))
        den_ref[:, hs] = jnp.broadcast_to(den, (ROWS, HEAD_DIM))
        wlk_sum = jnp.sum((wl * k_f).reshape(S_RB, S_T, 8, HEAD_DIM), axis=1).reshape(nb, HEAD_DIM)
        dl_seq = dl.reshape(S_RB, S_T, 8, 1)[:, 0].reshape(nb, 1)
        n_ref[:, hs] = dl_seq * n_old + wlk_sum

    og_ref[...] = jax.nn.sigmoid(_mm(a, wbig_ref[:, 1024:1536]))
    zg_ref[...] = _silu(_mm(a, wbig_ref[:, 512:1024]))

    u = _mm(a, wbig_ref[:, 1536:2048])
    _s5_in(u, wb_ref, xsre_s, xsim_s)
    sre_ref[...] = sre_in_ref[...]
    sim_ref[...] = sim_in_ref[...]
    for rb in range(S_RB):
        _s5_scan(xsre_s, xsim_s, are_ref, aim_ref, sre_ref, sim_ref, rb * S_BLK, slice(rb * 8, (rb + 1) * 8), S_T)
    y_s5 = jnp.concatenate([_s5_out(xsre_s, xsim_s, wcre_ref, wcim_ref, hf) for hf in range(2)], axis=-1)
    yb_ref[...] = _s5_post(y_s5, u, a, wbig_ref, wglu_ref, s5d_ref, bglu_ref)


def _sample_front_call(x_tm, weights, vecs, are8, aim8, n2, mrow_in, conv_tm, sre2, sim2):
    (wbig, wgate, wqk, wv, wb, wcre, wcim, wglu, _, _, _) = weights
    (lnmix, convw, convb, bi8, bf8, _, _, s5d, bglu, _, _) = vecs
    nb = S_RB * 8
    f32 = lambda *s: jax.ShapeDtypeStruct(s, F32)
    bf = lambda *s: jax.ShapeDtypeStruct(s, BF16)
    out_shape = (bf(ROWS, MIX), bf(ROWS, MIX), bf(ROWS, MIX), f32(ROWS, 128), f32(ROWS, MIX), f32(ROWS, MIX),
                 f32(ROWS, MIX), f32(ROWS, MIX), f32(ROWS, MIX), f32(ROWS, MIX), f32(ROWS, MIX),
                 f32(nb, MIX), f32(8, ROWS), f32(nb * (CONV_W - 1), MIX), f32(nb, N_STATE), f32(nb, N_STATE))
    scratch = [pltpu.VMEM((S_RB * (CONV_W - 1 + S_T) * 8, MIX), F32),
               pltpu.VMEM((ROWS, N_STATE), F32), pltpu.VMEM((ROWS, N_STATE), F32)]
    return pl.pallas_call(
        _sample_front_body, out_shape=out_shape, scratch_shapes=scratch,
        compiler_params=pltpu.CompilerParams(vmem_limit_bytes=VMEM_LIMIT),
        name="sample_front",
    )(x_tm, wbig, wgate, wqk, wv, wb, wcre, wcim, wglu, lnmix, convw, convb, bi8, bf8, s5d, bglu, are8, aim8,
      n2, mrow_in, conv_tm, sre2, sim2)


def _sample_state_body(c_ref, q_ref, k_ref, wlv_ref, colt_ref, qc_ref, cnew_ref):
    sub = lax.broadcasted_iota(jnp.int32, (S_BLK, HEAD_DIM), 0) % 8
    for sb in range(S_SB):
        rows = slice(sb * S_BLK, (sb + 1) * S_BLK)
        for h in range(N_HEADS):
            hs = slice(h * HEAD_DIM, (h + 1) * HEAD_DIM)
            q = q_ref[rows, hs]
            k = k_ref[rows, hs]
            wlv = wlv_ref[rows, hs]
            acc = jnp.zeros((S_BLK, HEAD_DIM), F32)
            for b8 in range(8):
                c_old = c_ref[0, sb * 8 + b8, h]
                sel = sub == b8
                acc = jnp.where(sel, _mm(q, c_old.astype(BF16)), acc)
                upd = lax.dot_general(k, jnp.where(sel, wlv, jnp.zeros_like(wlv)), (((0,), (0,)), ((), ())),
                                      preferred_element_type=F32)
                dl = colt_ref[sb * S_BLK + b8:sb * S_BLK + b8 + 1, COL_DL + h:COL_DL + h + 1]
                cnew_ref[0, sb * 8 + b8, h] = dl * c_old + upd
            qc_ref[rows, hs] = acc


S_SB = 2


def _sample_state_call(c_state, q_bf, k_bf, wlv_bf, colt):
    row_spec = lambda w: pl.BlockSpec((S_SB * S_BLK, w), lambda i: (i, 0))
    c_spec = pl.BlockSpec((1, S_SB * 8, N_HEADS, HEAD_DIM, HEAD_DIM), lambda i: (0, i, 0, 0, 0))
    return pl.pallas_call(
        _sample_state_body,
        grid=(S_RB // S_SB,),
        in_specs=[c_spec, row_spec(MIX), row_spec(MIX), row_spec(MIX), row_spec(128)],
        out_specs=(row_spec(MIX), c_spec),
        out_shape=(jax.ShapeDtypeStruct((ROWS, MIX), F32), jax.ShapeDtypeStruct(c_state.shape, F32)),
        compiler_params=pltpu.CompilerParams(dimension_semantics=("arbitrary",), vmem_limit_bytes=VMEM_LIMIT),
        name="sample_state",
    )(c_state, q_bf, k_bf, wlv_bf, colt)


def _sample_back_body(x_ref, p_ref, num_ref, dec_ref, den_ref, qc_ref, og_ref, zg_ref, xc_ref, yb_ref,
                      wout_ref, wple_ref, wpg_ref, lnhead_ref, skip_ref, lnple_ref, lnfin_ref, y_ref):
    ha = (num_ref[...] + dec_ref[...] * qc_ref[...]) / den_ref[...] * og_ref[...]
    ha = jnp.concatenate([_head_norm(ha[:, h * HEAD_DIM:(h + 1) * HEAD_DIM]) for h in range(N_HEADS)], axis=-1)
    ha = (ha * lnhead_ref[...] + skip_ref[...] * xc_ref[...]) * zg_ref[...]
    y_ref[...] = _merge_out(x_ref[...], p_ref[...], ha, yb_ref[...], wout_ref, wple_ref, wpg_ref,
                            lnple_ref, lnfin_ref)


def _sample_back_call(x_tm, p_tm, num, dec, den, qc, og, zg, xc, yb, weights, vecs):
    (_, _, _, _, _, _, _, _, wout, wple, wpg) = weights
    (_, _, _, _, _, lnhead, skip, _, _, lnple, lnfin) = vecs
    return pl.pallas_call(
        _sample_back_body, out_shape=jax.ShapeDtypeStruct((ROWS, D_MODEL), F32),
        compiler_params=pltpu.CompilerParams(vmem_limit_bytes=VMEM_LIMIT),
        name="sample_back",
    )(x_tm, p_tm, num, dec, den, qc, og, zg, xc, yb, wout, wple, wpg, lnhead, skip, lnple, lnfin)


def _to_blocks(a, n_t):
    nb, _, w = a.shape
    return jnp.transpose(a.reshape(nb // 8, 8, n_t, w), (0, 2, 1, 3)).reshape(nb * n_t, w)


def _from_blocks(a, n_t):
    rows, w = a.shape
    nb = rows // n_t
    return jnp.transpose(a.reshape(nb // 8, n_t, 8, w), (0, 2, 1, 3)).reshape(nb, n_t, w)


def _sample_path(x_sample, p_sample, c_state, n_state, m_state, conv_state, sre_state, sim_state,
                 weights, vecs, are8, aim8):
    nb, n_t, _ = x_sample.shape
    assert nb == S_RB * 8 and n_t == S_T
    x_tm = _to_blocks(x_sample, S_T)
    p_tm = _to_blocks(p_sample, S_T)
    conv_tm = _to_blocks(conv_state, CONV_W - 1)
    m_rows = jnp.broadcast_to(jnp.transpose(m_state).reshape(N_HEADS, S_RB, 1, 8), (N_HEADS, S_RB, S_T, 8))
    mrow_in = jnp.pad(m_rows.reshape(N_HEADS, ROWS), ((0, 8 - N_HEADS), (0, 0)))
    (q_bf, k_bf, wlv_bf, colt, num, dec, den, og, zg, xc, yb, n_new, mrow, conv_new, sre, sim) = _sample_front_call(
        x_tm, weights, vecs, are8, aim8, n_state.reshape(nb, MIX), mrow_in, conv_tm,
        sre_state.reshape(nb, N_STATE), sim_state.reshape(nb, N_STATE))
    qc, c_new = _sample_state_call(c_state, q_bf, k_bf, wlv_bf, colt)
    y_tm = _sample_back_call(x_tm, p_tm, num, dec, den, qc, og, zg, xc, yb, weights, vecs)
    m_new = jnp.transpose(mrow[0:N_HEADS].reshape(N_HEADS, S_RB, S_T, 8)[:, :, 0, :].reshape(N_HEADS, nb))
    return (_from_blocks(y_tm, S_T), c_new, n_new.reshape(1, nb, N_HEADS, HEAD_DIM), m_new[None],
            _from_blocks(conv_new, CONV_W - 1)[None], sre.reshape(1, nb, N_GROUPS, S5_STATE),
            sim.reshape(1, nb, N_GROUPS, S5_STATE))


def _layout_params(ln_mix, w_in, b_igate, b_fgate, conv_w, conv_b, w_q, w_k, w_v, ln_head, skip_a,
                   s5_B_re, s5_B_im, s5_C_re, s5_C_im, s5_D, w_glu, b_glu, w_out, w_ple, ln_ple,
                   w_ple_gate, ln_final):
    w = w_in[0]
    o_i = 3 * MIX
    o_f = o_i + N_HEADS
    o_s = o_f + N_HEADS
    wbig = jnp.concatenate([w[:, 0:3 * MIX], w[:, o_s:o_s + 2 * MIX]], axis=1).astype(BF16)
    wgate = jnp.pad(jnp.concatenate([w[:, o_i:o_f], jnp.zeros((D_MODEL, 8 - N_HEADS), F32), w[:, o_f:o_s]], axis=1),
                    ((0, 0), (0, 128 - 8 - N_HEADS))).astype(BF16)
    wqk = jnp.concatenate([w_q[0], w_k[0]], axis=-1).astype(BF16)
    wv = w_v[0].astype(BF16)
    eye = jnp.eye(16, dtype=F32)

    def c_blocks(cmat):
        cr = cmat.reshape(2, 16, S5_GROUP, S5_STATE)
        return jnp.einsum('ab,hbcp->hapbc', eye, cr).reshape(2, 16 * S5_STATE, 16 * S5_GROUP).astype(BF16)

    def b_tiled(bmat):
        bt = jnp.transpose(bmat.reshape(2, 16, S5_STATE, S5_GROUP), (0, 1, 3, 2)).reshape(2, 256, 1, S5_STATE)
        return jnp.broadcast_to(bt, (2, 256, 16, S5_STATE)).reshape(2, 256, 16 * S5_STATE)

    weights = [wbig, wgate, wqk, wv, None, c_blocks(s5_C_re[0]), c_blocks(s5_C_im[0]),
               w_glu[0].astype(BF16), w_out[0].astype(BF16), w_ple[0].astype(BF16), w_ple_gate[0].astype(BF16)]
    pad8 = lambda v: jnp.pad(v[0][:, None], ((0, 8 - N_HEADS), (0, 0)))
    vecs = [ln_mix[0][None, :], conv_w[0], conv_b[0][None, :], pad8(b_igate), pad8(b_fgate),
            ln_head[0][None, :], skip_a[0][None, :], s5_D[0][None, :], b_glu[0][None, :],
            ln_ple[0][None, :], ln_final[None, :]]
    return weights, vecs, b_tiled(s5_B_re[0]), b_tiled(s5_B_im[0])


def kernel(x_prompt, x_sample, p_prompt, p_sample, state_mlstm_C, state_mlstm_n, state_mlstm_m, state_conv, state_s5_re, state_s5_im, ln_mix, w_in, b_igate, b_fgate, conv_w, conv_b, w_q, w_k, w_v, ln_head, skip_a, s5_lam_re, s5_lam_im, s5_log_dt, s5_B_re, s5_B_im, s5_C_re, s5_C_im, s5_D, w_glu, b_glu, w_out, w_ple, ln_ple, w_ple_gate, ln_final):
    weights, vecs, bt_re, bt_im = _layout_params(
        ln_mix, w_in, b_igate, b_fgate, conv_w, conv_b, w_q, w_k, w_v, ln_head, skip_a,
        s5_B_re, s5_B_im, s5_C_re, s5_C_im, s5_D, w_glu, b_glu, w_out, w_ple, ln_ple, w_ple_gate, ln_final)
    are8, aim8, wb = _prep_call(s5_lam_re.reshape(1, N_STATE), s5_lam_im.reshape(1, N_STATE),
                                s5_log_dt.reshape(1, N_STATE), bt_re, bt_im)
    weights[4] = wb

    nbp = x_prompt.shape[0]
    y_p, c_p, n_p, mrow_p, conv_p, sre_p, sim_p = _prompt_call(x_prompt, p_prompt[0], weights, vecs, are8, aim8)
    pn = n_p.reshape(1, nbp, N_HEADS, HEAD_DIM)
    pm = jnp.transpose(mrow_p[0:N_HEADS, ::P_T])[None]
    pre = sre_p.reshape(1, nbp, N_GROUPS, S5_STATE)
    pim = sim_p.reshape(1, nbp, N_GROUPS, S5_STATE)

    y_s, c_s, n_s, m_s, conv_s, sre_s, sim_s = _sample_path(
        x_sample, p_sample[0], state_mlstm_C, state_mlstm_n[0], state_mlstm_m[0], state_conv[0],
        state_s5_re[0], state_s5_im[0], weights, vecs, are8, aim8)
    return (y_p, y_s, c_p, pn, pm, conv_p, pre, pim, c_s, n_s, m_s, conv_s, sre_s, sim_s)
```

```python
import functools
import math

import jax
import jax.numpy as jnp
from jax import lax
from jax.experimental import pallas as pl
from jax.experimental.pallas import tpu as pltpu

F32 = jnp.float32
BF16 = jnp.bfloat16

D_MODEL = 1024
D_PLE = 256
MIX = 512
N_HEADS = 4
HEAD_DIM = 128
N_GROUPS = 32
S5_GROUP = 16
S5_STATE = 64
N_STATE = N_GROUPS * S5_STATE
CONV_W = 4
EPS = 1e-6
K_SCALE = HEAD_DIM ** -0.5
ROWS = 512
GROUP_ROWS = 256
VMEM_LIMIT = 56 * 1024 * 1024


def _mm(a, w):
    return jnp.dot(a, w, preferred_element_type=F32)


def _rms(x, g):
    ms = jnp.mean(x * x, axis=-1, keepdims=True)
    return x * lax.rsqrt(ms + EPS) * g


def _sigmoid(x):
    return 0.5 * jnp.tanh(0.5 * x) + 0.5


def _silu(x):
    h = 0.5 * x
    return h + h * jnp.tanh(h)


def _gelu_tanh(x):
    c = math.sqrt(2.0 / math.pi)
    return 0.5 * x * (1.0 + jnp.tanh(c * (x + 0.044715 * (x * x * x))))


def _log_sigmoid(x):
    return jnp.minimum(x, 0.0) - jnp.log1p(jnp.exp(-jnp.abs(x)))


SCAN_RADIX = 8


def _seg_scan(x, tidx, n_t, t_stride, op, fill):
    span = 1
    while span < n_t:
        acc = x
        for j in range(1, SCAN_RADIX):
            if j * span >= n_t:
                break
            prev = pltpu.roll(x, j * span * t_stride, axis=1)
            acc = op(acc, jnp.where(tidx >= j * span, prev, fill))
        x = acc
        span *= SCAN_RADIX
    return x


def _bcast_last(x, tidx, n_t, t_stride):
    y = jnp.where(tidx == n_t - 1, x, 0.0)
    span = 1
    while span < n_t:
        acc = y
        for j in range(1, SCAN_RADIX):
            if j * span >= n_t:
                break
            nxt = pltpu.roll(y, ROWS - j * span * t_stride, axis=1)
            acc = acc + jnp.where(tidx + j * span <= n_t - 1, nxt, 0.0)
        y = acc
        span *= SCAN_RADIX
    return y


def _gate_rows(a, wg_ref, bi8, bf8, mprev, n_t, t_stride):
    gt = lax.dot_general(wg_ref[...], a, (((1,), (1,)), ((), ())), preferred_element_type=F32)
    lane = lax.broadcasted_iota(jnp.int32, (8, ROWS), 1)
    tidx = (lane // t_stride) % n_t
    i8 = gt[0:8] + bi8
    logf = _log_sigmoid(gt[8:16] + bf8)
    b = _seg_scan(logf, tidx, n_t, t_stride, jnp.add, 0.0)
    c = i8 - b
    cm = _seg_scan(c, tidx, n_t, t_stride, jnp.maximum, -jnp.inf)
    big_m = jnp.maximum(mprev, cm)
    mt = b + big_m
    m_last = _bcast_last(big_m, tidx, n_t, t_stride)
    mnew = _bcast_last(mt, tidx, n_t, t_stride)
    stack = jnp.concatenate(
        [-big_m, jnp.exp(mprev - big_m), jnp.exp(-mt), jnp.exp(c - m_last), jnp.exp(mprev - m_last),
         jnp.zeros((128 - 40, ROWS), F32)], axis=0)
    return c, stack.T, mnew


COL_NEGM, COL_DECAY, COL_EM, COL_WL, COL_DL = 0, 8, 16, 24, 32


def _intra(qg, kg, vg, col_a, row_b, mask):
    s = lax.dot_general(qg, kg, (((1,), (1,)), ((), ())), preferred_element_type=F32)
    p = s * jnp.exp(jnp.where(mask, col_a + row_b, -jnp.inf))
    rs = jnp.sum(p, axis=-1, keepdims=True)
    return _mm(p.astype(BF16), vg), rs


def _head_norm(x):
    mu = jnp.mean(x, axis=-1, keepdims=True)
    xc = x - mu
    var = jnp.mean(xc * xc, axis=-1, keepdims=True)
    return xc * lax.rsqrt(var + EPS)


def _s5_scan(bure_s, buim_s, xsre_s, xsim_s, are_ref, aim_ref, sre_ref, sim_ref, row_base, st_rows, n_t):
    chunk = 512
    assert n_t % 2 == 0 and row_base % 16 == 0
    for ch in range(N_STATE // chunk):
        ls = slice(ch * chunk, (ch + 1) * chunk)
        ar = are_ref[:, ls]
        ai = aim_ref[:, ls]
        xr = sre_ref[st_rows, ls]
        xi = sim_ref[st_rows, ls]
        for t in range(0, n_t, 2):
            r0 = row_base + t * 8
            xr1, xi1 = (ar * xr - ai * xi + bure_s[r0:r0 + 8, ls], ar * xi + ai * xr + buim_s[r0:r0 + 8, ls])
            xr, xi = (ar * xr1 - ai * xi1 + bure_s[r0 + 8:r0 + 16, ls],
                      ar * xi1 + ai * xr1 + buim_s[r0 + 8:r0 + 16, ls])
            xsre_s[r0:r0 + 16, ls] = jnp.concatenate([xr1, xr], axis=0).astype(BF16)
            xsim_s[r0:r0 + 16, ls] = jnp.concatenate([xi1, xi], axis=0).astype(BF16)
        sre_ref[st_rows, ls] = xr
        sim_ref[st_rows, ls] = xi


def _s5_in(u_tm, wb_ref, xsre_s, xsim_s):
    half = N_STATE // 2
    for hf in range(2):
        res = _mm(u_tm[:, hf * 256:(hf + 1) * 256].astype(BF16), wb_ref[hf])
        xsre_s[:, hf * half:(hf + 1) * half] = res[:, :half]
        xsim_s[:, hf * half:(hf + 1) * half] = res[:, half:]


def _s5_out(xsre_s, xsim_s, wcre_ref, wcim_ref, hf):
    half = N_STATE // 2
    hs = slice(hf * half, (hf + 1) * half)
    return _mm(xsre_s[:, hs], wcre_ref[hf]) - _mm(xsim_s[:, hs], wcim_ref[hf])


def _s5_post(y_s5, u, a, wbig_ref, wglu_ref, s5d_ref, bglu_ref):
    yb = _gelu_tanh(y_s5 + s5d_ref[...] * u)
    yb = yb * _sigmoid(_mm(yb.astype(BF16), wglu_ref[...]) + bglu_ref[...])
    return yb * _silu(_mm(a, wbig_ref[:, 2048:2560]))


OUT_BLOCK = 256


def _merge_out(x, p, h_a, y_b, wout_ref, wple_ref, wpg_ref, lnple_ref, lnfin_ref):
    h1 = x + _mm(h_a.astype(BF16), wout_ref[0:MIX, :]) + _mm(y_b.astype(BF16), wout_ref[MIX:2 * MIX, :])
    e_half = 0.5 * _rms(_mm(p.astype(BF16), wple_ref[...]), lnple_ref[...])
    h1_bf = h1.astype(BF16)
    base = h1 + e_half
    blocks = []
    ssq = jnp.zeros((h1.shape[0], 1), F32)
    for nb in range(D_MODEL // OUT_BLOCK):
        cs = slice(nb * OUT_BLOCK, (nb + 1) * OUT_BLOCK)
        h2 = base[:, cs] + e_half[:, cs] * jnp.tanh(0.5 * _mm(h1_bf, wpg_ref[:, cs]))
        ssq = ssq + jnp.sum(h2 * h2, axis=-1, keepdims=True)
        blocks.append(h2)
    scale = lax.rsqrt(ssq * (1.0 / D_MODEL) + EPS)
    return jnp.concatenate(blocks, axis=-1) * scale * lnfin_ref[...]


def _prep_body(lr_ref, li_ref, ldt_ref, btr_ref, bti_ref, are_ref, aim_ref, wb_ref):
    lr = jnp.minimum(lr_ref[...], -1e-4)
    li = li_ref[...]
    dt = jnp.exp(ldt_ref[...])
    mag = jnp.exp(lr * dt)
    a_re = mag * jnp.cos(li * dt)
    a_im = mag * jnp.sin(li * dt)
    den = lr * lr + li * li
    xr = a_re - 1.0
    g_re = (xr * lr + a_im * li) / den
    g_im = (a_im * lr - xr * li) / den
    are_ref[...] = jnp.broadcast_to(a_re, (8, N_STATE))
    aim_ref[...] = jnp.broadcast_to(a_im, (8, N_STATE))
    half = N_STATE // 2
    row = lax.broadcasted_iota(jnp.int32, (256, half), 0)
    col = lax.broadcasted_iota(jnp.int32, (256, half), 1)
    mask = (row // S5_GROUP) == (col // S5_STATE)
    for hf in range(2):
        gr = g_re[:, hf * half:(hf + 1) * half]
        gi = g_im[:, hf * half:(hf + 1) * half]
        br = btr_ref[hf]
        bi = bti_ref[hf]
        wb_ref[hf, :, 0:half] = jnp.where(mask, gr * br - gi * bi, 0.0).astype(BF16)
        wb_ref[hf, :, half:2 * half] = jnp.where(mask, gr * bi + gi * br, 0.0).astype(BF16)


def _prep_call(lam_re, lam_im, log_dt, bt_re, bt_im):
    return pl.pallas_call(
        _prep_body,
        out_shape=(jax.ShapeDtypeStruct((8, N_STATE), F32), jax.ShapeDtypeStruct((8, N_STATE), F32),
                   jax.ShapeDtypeStruct((2, 256, N_STATE), BF16)),
        name="s5_prep",
    )(lam_re, lam_im, log_dt, bt_re, bt_im)


P_T = 64
P_NB = 8


def _prompt_body(x_ref, p_ref, wbig_ref, wgate_ref, wqk_ref, wv_ref, wb_ref, wcre_ref, wcim_ref,
                 wglu_ref, wout_ref, wple_ref, wpg_ref, lnmix_ref, convw_ref, convb_ref, bi8_ref, bf8_ref,
                 lnhead_ref, skip_ref, s5d_ref, bglu_ref, lnple_ref, lnfin_ref, are_ref, aim_ref,
                 y_ref, c_ref, n_ref, mrow_ref, conv_ref, sre_ref, sim_ref,
                 ext_s, ha_s, utm_s, ytm_s, ybm_s, bure_s, buim_s, xsre_s, xsim_s):
    step = pl.program_id(0)

    @pl.when(step == 0)
    def _init():
        c_ref[...] = jnp.zeros_like(c_ref)
        n_ref[...] = jnp.zeros_like(n_ref)
        mrow_ref[...] = jnp.zeros_like(mrow_ref)
        sre_ref[...] = jnp.zeros_like(sre_ref)
        sim_ref[...] = jnp.zeros_like(sim_ref)
        ext_s[:, 0:8, :] = jnp.zeros((P_NB, 8, MIX), F32)

    x = x_ref[...].reshape(ROWS, D_MODEL)
    a = _rms(x, lnmix_ref[...]).astype(BF16)

    c8, colt, mnew = _gate_rows(a, wgate_ref, bi8_ref[...], bf8_ref[...], mrow_ref[...], P_T, 1)
    mrow_ref[...] = mnew

    xm = _mm(a, wbig_ref[:, 0:512])
    ext_s[:, 8:8 + P_T, :] = xm.reshape(P_NB, P_T, MIX)
    yc = convb_ref[...]
    for j in range(CONV_W):
        yc = yc + ext_s[:, 5 + j:5 + j + P_T, :].reshape(ROWS, MIX) * convw_ref[j:j + 1, :]
    tail = ext_s[:, 5 + P_T:8 + P_T, :]
    conv_ref[0] = tail
    ext_s[:, 5:8, :] = tail
    xc = _silu(yc)
    xc_bf = xc.astype(BF16)
    xm_bf = xm.astype(BF16)

    r = lax.broadcasted_iota(jnp.int32, (GROUP_ROWS, GROUP_ROWS), 0)
    cc = lax.broadcasted_iota(jnp.int32, (GROUP_ROWS, GROUP_ROWS), 1)
    mask = ((r // P_T) == (cc // P_T)) & (cc <= r)

    for h in range(N_HEADS):
        hs = slice(h * HEAD_DIM, (h + 1) * HEAD_DIM)
        qk = _mm(xc_bf[:, hs], wqk_ref[h])
        q_f = qk[:, :HEAD_DIM]
        k_f = qk[:, HEAD_DIM:] * K_SCALE
        v_f = _mm(xm_bf[:, hs], wv_ref[h])
        q_bf = q_f.astype(BF16)
        k_bf = k_f.astype(BF16)
        v_bf = v_f.astype(BF16)
        for g in range(ROWS // GROUP_ROWS):
            gs = slice(g * GROUP_ROWS, (g + 1) * GROUP_ROWS)
            decay = colt[gs, COL_DECAY + h:COL_DECAY + h + 1]
            wl = colt[gs, COL_WL + h:COL_WL + h + 1]
            dl = colt[gs, COL_DL + h:COL_DL + h + 1]
            num, rs = _intra(q_bf[gs], k_bf[gs], v_bf[gs], colt[gs, COL_NEGM + h:COL_NEGM + h + 1],
                             c8[h:h + 1, gs], mask)
            wlv = (wl * v_f[gs]).astype(BF16)
            wlk = wl * k_f[gs]
            qcs, qns = [], []
            for bb in range(GROUP_ROWS // P_T):
                b = g * (GROUP_ROWS // P_T) + bb
                rb = slice(g * GROUP_ROWS + bb * P_T, g * GROUP_ROWS + (bb + 1) * P_T)
                rl = slice(bb * P_T, (bb + 1) * P_T)
                c_old = c_ref[0, b, h]
                n_old = n_ref[b:b + 1, hs]
                qcs.append(_mm(q_bf[rb], c_old.astype(BF16)))
                qns.append(jnp.sum(q_f[rb] * n_old, axis=-1, keepdims=True))
                upd = lax.dot_general(k_bf[rb], wlv[rl], (((0,), (0,)), ((), ())), preferred_element_type=F32)
                dlb = dl[rl]
                c_ref[0, b, h] = jnp.concatenate([dlb, dlb], axis=0) * c_old + upd
                n_ref[b:b + 1, hs] = dlb[0:1] * n_old + jnp.sum(wlk[rl], axis=0, keepdims=True)
            num = num + decay * jnp.concatenate(qcs, axis=0)
            qn = rs + decay * jnp.concatenate(qns, axis=0)
            den = jnp.maximum(jnp.abs(qn), colt[gs, COL_EM + h:COL_EM + h + 1])
            ha_s[gs, hs] = num / den

    ha = ha_s[...] * _sigmoid(_mm(a, wbig_ref[:, 1024:1536]))
    ha = jnp.concatenate([_head_norm(ha[:, h * HEAD_DIM:(h + 1) * HEAD_DIM]) for h in range(N_HEADS)], axis=-1)
    ha = (ha * lnhead_ref[...] + skip_ref[...] * xc) * _silu(_mm(a, wbig_ref[:, 512:1024]))

    u = _mm(a, wbig_ref[:, 1536:2048])
    for b in range(P_NB):
        for j in range(P_T // 8):
            for lc in range(MIX // 128):
                utm_s[lc, pl.ds(j * 64 + b, 8, stride=8), :] = (
                    u[b * P_T + j * 8:b * P_T + j * 8 + 8, lc * 128:(lc + 1) * 128])
    _s5_in(jnp.concatenate([utm_s[lc] for lc in range(MIX // 128)], axis=-1), wb_ref, bure_s, buim_s)
    _s5_scan(bure_s, buim_s, xsre_s, xsim_s, are_ref, aim_ref, sre_ref, sim_ref, 0, slice(0, 8), P_T)
    for hf in range(2):
        y_half = _s5_out(xsre_s, xsim_s, wcre_ref, wcim_ref, hf)
        ytm_s[2 * hf] = y_half[:, 0:128]
        ytm_s[2 * hf + 1] = y_half[:, 128:256]
    for b in range(P_NB):
        for j in range(P_T // 8):
            for lc in range(MIX // 128):
                ybm_s[b * P_T + j * 8:b * P_T + j * 8 + 8, lc * 128:(lc + 1) * 128] = (
                    ytm_s[lc, pl.ds(j * 64 + b, 8, stride=8), :])
    yb = _s5_post(ybm_s[...], u, a, wbig_ref, wglu_ref, s5d_ref, bglu_ref)

    p = p_ref[...].reshape(ROWS, D_PLE)
    y = _merge_out(x, p, ha, yb, wout_ref, wple_ref, wpg_ref, lnple_ref, lnfin_ref)
    y_ref[...] = y.reshape(P_NB, P_T, D_MODEL)


def _const_spec(shape):
    nd = len(shape)
    return pl.BlockSpec(shape, lambda i, nd=nd: (0,) * nd, pipeline_mode=pl.Buffered(1))


def _prompt_call(x, p, weights, vecs, are8, aim8):
    nb, seq, _ = x.shape
    assert nb == P_NB and seq % P_T == 0
    n_steps = seq // P_T
    w_specs = [_const_spec(w.shape) for w in weights]
    v_specs = [_const_spec(v.shape) for v in vecs]
    in_specs = ([pl.BlockSpec((P_NB, P_T, D_MODEL), lambda i: (0, i, 0)),
                 pl.BlockSpec((P_NB, P_T, D_PLE), lambda i: (0, i, 0))]
                + w_specs + v_specs + [_const_spec((8, N_STATE)), _const_spec((8, N_STATE))])
    out_shape = (
        jax.ShapeDtypeStruct((nb, seq, D_MODEL), F32),
        jax.ShapeDtypeStruct((1, nb, N_HEADS, HEAD_DIM, HEAD_DIM), F32),
        jax.ShapeDtypeStruct((nb, MIX), F32),
        jax.ShapeDtypeStruct((8, ROWS), F32),
        jax.ShapeDtypeStruct((1, nb, CONV_W - 1, MIX), F32),
        jax.ShapeDtypeStruct((8, N_STATE), F32),
        jax.ShapeDtypeStruct((8, N_STATE), F32),
    )
    out_specs = (
        pl.BlockSpec((P_NB, P_T, D_MODEL), lambda i: (0, i, 0)),
        pl.BlockSpec((1, nb, N_HEADS, HEAD_DIM, HEAD_DIM), lambda i: (0, 0, 0, 0, 0)),
        pl.BlockSpec((nb, MIX), lambda i: (0, 0)),
        pl.BlockSpec((8, ROWS), lambda i: (0, 0)),
        pl.BlockSpec((1, nb, CONV_W - 1, MIX), lambda i: (0, 0, 0, 0)),
        pl.BlockSpec((8, N_STATE), lambda i: (0, 0)),
        pl.BlockSpec((8, N_STATE), lambda i: (0, 0)),
    )
    scratch = [
        pltpu.VMEM((P_NB, P_T + 8, MIX), F32),
        pltpu.VMEM((ROWS, MIX), F32),
        pltpu.VMEM((MIX // 128, ROWS, 128), F32),
        pltpu.VMEM((MIX // 128, ROWS, 128), F32),
        pltpu.VMEM((ROWS, MIX), F32),
        pltpu.VMEM((ROWS, N_STATE), F32),
        pltpu.VMEM((ROWS, N_STATE), F32),
        pltpu.VMEM((ROWS, N_STATE), BF16),
        pltpu.VMEM((ROWS, N_STATE), BF16),
    ]
    return pl.pallas_call(
        _prompt_body,
        grid=(n_steps,),
        in_specs=in_specs,
        out_specs=out_specs,
        out_shape=out_shape,
        scratch_shapes=scratch,
        compiler_params=pltpu.CompilerParams(dimension_semantics=("arbitrary",), vmem_limit_bytes=VMEM_LIMIT),
        name="prompt_layer",
    )(x, p, *weights, *vecs, are8, aim8)


S_T = 4
S_RB = ROWS // (8 * S_T)
S_BLK = 8 * S_T


def _sample_front_body(x_ref, wbig_ref, wgate_ref, wqk_ref, wv_ref, wb_ref, wcre_ref, wcim_ref, wglu_ref,
                       lnmix_ref, convw_ref, convb_ref, bi8_ref, bf8_ref, s5d_ref, bglu_ref, are_ref, aim_ref,
                       n_in_ref, mrow_in_ref, conv_in_ref, sre_in_ref, sim_in_ref,
                       q_ref, k_ref, wlv_ref, colt_ref, num_ref, dec_ref, den_ref, og_ref, zg_ref, xc_ref, yb_ref,
                       n_ref, mrow_ref, conv_ref, sre_ref, sim_ref,
                       ext_s, bure_s, buim_s, xsre_s, xsim_s):
    x = x_ref[...]
    a = _rms(x, lnmix_ref[...]).astype(BF16)

    xm = _mm(a, wbig_ref[:, 0:512])
    ext_rows = (CONV_W - 1 + S_T) * 8
    ycs = []
    for rb in range(S_RB):
        ext_s[rb * ext_rows:rb * ext_rows + 24, :] = conv_in_ref[rb * 24:(rb + 1) * 24, :]
        ext_s[rb * ext_rows + 24:(rb + 1) * ext_rows, :] = xm[rb * S_BLK:(rb + 1) * S_BLK, :]
        yc = convb_ref[...]
        for j in range(CONV_W):
            yc = yc + ext_s[rb * ext_rows + 8 * j:rb * ext_rows + 8 * j + S_BLK, :] * convw_ref[j:j + 1, :]
        ycs.append(yc)
        conv_ref[rb * 24:(rb + 1) * 24, :] = ext_s[rb * ext_rows + S_BLK:(rb + 1) * ext_rows, :]
    xc = _silu(jnp.concatenate(ycs, axis=0))
    xc_ref[...] = xc
    xc_bf = xc.astype(BF16)
    xm_bf = xm.astype(BF16)

    mprev = mrow_in_ref[...]
    c8, colt, mnew = _gate_rows(a, wgate_ref, bi8_ref[...], bf8_ref[...], mprev, S_T, 8)
    mrow_ref[...] = mnew
    colt_ref[...] = colt

    r = lax.broadcasted_iota(jnp.int32, (GROUP_ROWS, GROUP_ROWS), 0)
    cc = lax.broadcasted_iota(jnp.int32, (GROUP_ROWS, GROUP_ROWS), 1)
    same_seq = ((r // S_BLK) == (cc // S_BLK)) & ((r % 8) == (cc % 8))
    mask = same_seq & (((cc // 8) % S_T) <= ((r // 8) % S_T))

    nb = S_RB * 8
    for h in range(N_HEADS):
        hs = slice(h * HEAD_DIM, (h + 1) * HEAD_DIM)
        qk = _mm(xc_bf[:, hs], wqk_ref[h])
        q_f = qk[:, :HEAD_DIM]
        k_f = qk[:, HEAD_DIM:] * K_SCALE
        v_f = _mm(xm_bf[:, hs], wv_ref[h])
        q_bf = q_f.astype(BF16)
        k_bf = k_f.astype(BF16)
        v_bf = v_f.astype(BF16)
        decay = colt[:, COL_DECAY + h:COL_DECAY + h + 1]
        wl = colt[:, COL_WL + h:COL_WL + h + 1]
        dl = colt[:, COL_DL + h:COL_DL + h + 1]
        nums, rss = [], []
        for g in range(ROWS // GROUP_ROWS):
            gs = slice(g * GROUP_ROWS, (g + 1) * GROUP_ROWS)
            num, rs = _intra(q_bf[gs], k_bf[gs], v_bf[gs], colt[gs, COL_NEGM + h:COL_NEGM + h + 1],
                             c8[h:h + 1, gs], mask)
            nums.append(num)
            rss.append(rs)
        q_ref[:, hs] = q_bf
        k_ref[:, hs] = k_bf
        wlv_ref[:, hs] = (wl * v_f).astype(BF16)
        num_ref[:, hs] = jnp.concatenate(nums, axis=0)
        n_old = n_in_ref[:, hs]
        n_rows = jnp.broadcast_to(n_old.reshape(S_RB, 1, 8, HEAD_DIM), (S_RB, S_T, 8, HEAD_DIM)).reshape(ROWS, HEAD_DIM)
        qn = jnp.concatenate(rss, axis=0) + decay * jnp.sum(q_f * n_rows, axis=-1, keepdims=True)
        den = jnp.maximum(jnp.abs(qn), colt[:, COL_EM + h:COL_EM + h + 1])
        dec_ref[:, hs] = jnp.broadcast_to(decay, (ROWS, HEAD_DIM))
        den_ref[:, hs] = jnp.broadcast_to(den, (ROWS, HEAD_DIM))
        wlk_sum = jnp.sum((wl * k_f).reshape(S_RB, S_T, 8, HEAD_DIM), axis=1).reshape(nb, HEAD_DIM)
        dl_seq = dl.reshape(S_RB, S_T, 8, 1)[:, 0].reshape(nb, 1)
        n_ref[:, hs] = dl_seq * n_old + wlk_sum

    og_ref[...] = _sigmoid(_mm(a, wbig_ref[:, 1024:1536]))
    zg_ref[...] = _silu(_mm(a, wbig_ref[:, 512:1024]))

    u = _mm(a, wbig_ref[:, 1536:2048])
    _s5_in(u, wb_ref, bure_s, buim_s)
    sre_ref[...] = sre_in_ref[...]
    sim_ref[...] = sim_in_ref[...]
    for rb in range(S_RB):
        _s5_scan(bure_s, buim_s, xsre_s, xsim_s, are_ref, aim_ref, sre_ref, sim_ref, rb * S_BLK,
                 slice(rb * 8, (rb + 1) * 8), S_T)
    y_s5 = jnp.concatenate([_s5_out(xsre_s, xsim_s, wcre_ref, wcim_ref, hf) for hf in range(2)], axis=-1)
    yb_ref[...] = _s5_post(y_s5, u, a, wbig_ref, wglu_ref, s5d_ref, bglu_ref)


def _sample_front_call(x_tm, weights, vecs, are8, aim8, n2, mrow_in, conv_tm, sre2, sim2):
    (wbig, wgate, wqk, wv, wb, wcre, wcim, wglu, _, _, _) = weights
    (lnmix, convw, convb, bi8, bf8, _, _, s5d, bglu, _, _) = vecs
    nb = S_RB * 8
    f32 = lambda *s: jax.ShapeDtypeStruct(s, F32)
    bf = lambda *s: jax.ShapeDtypeStruct(s, BF16)
    out_shape = (bf(ROWS, MIX), bf(ROWS, MIX), bf(ROWS, MIX), f32(ROWS, 128), f32(ROWS, MIX), f32(ROWS, MIX),
                 f32(ROWS, MIX), f32(ROWS, MIX), f32(ROWS, MIX), f32(ROWS, MIX), f32(ROWS, MIX),
                 f32(nb, MIX), f32(8, ROWS), f32(nb * (CONV_W - 1), MIX), f32(nb, N_STATE), f32(nb, N_STATE))
    scratch = [pltpu.VMEM((S_RB * (CONV_W - 1 + S_T) * 8, MIX), F32),
               pltpu.VMEM((ROWS, N_STATE), F32), pltpu.VMEM((ROWS, N_STATE), F32),
               pltpu.VMEM((ROWS, N_STATE), BF16), pltpu.VMEM((ROWS, N_STATE), BF16)]
    return pl.pallas_call(
        _sample_front_body, out_shape=out_shape, scratch_shapes=scratch,
        compiler_params=pltpu.CompilerParams(vmem_limit_bytes=VMEM_LIMIT),
        name="sample_front",
    )(x_tm, wbig, wgate, wqk, wv, wb, wcre, wcim, wglu, lnmix, convw, convb, bi8, bf8, s5d, bglu, are8, aim8,
      n2, mrow_in, conv_tm, sre2, sim2)


def _sample_state_body(c_ref, q_ref, k_ref, wlv_ref, colt_ref, qc_ref, cnew_ref):
    sub = lax.broadcasted_iota(jnp.int32, (S_BLK, HEAD_DIM), 0) % 8
    for sb in range(S_SB):
        rows = slice(sb * S_BLK, (sb + 1) * S_BLK)
        for h in range(N_HEADS):
            hs = slice(h * HEAD_DIM, (h + 1) * HEAD_DIM)
            q = q_ref[rows, hs]
            k = k_ref[rows, hs]
            wlv = wlv_ref[rows, hs]
            acc = jnp.zeros((S_BLK, HEAD_DIM), F32)
            for b8 in range(8):
                c_old = c_ref[0, sb * 8 + b8, h]
                sel = sub == b8
                acc = jnp.where(sel, _mm(q, c_old.astype(BF16)), acc)
                upd = lax.dot_general(k, jnp.where(sel, wlv, jnp.zeros_like(wlv)), (((0,), (0,)), ((), ())),
                                      preferred_element_type=F32)
                dl = colt_ref[sb * S_BLK + b8:sb * S_BLK + b8 + 1, COL_DL + h:COL_DL + h + 1]
                cnew_ref[0, sb * 8 + b8, h] = dl * c_old + upd
            qc_ref[rows, hs] = acc


S_SB = 2


def _sample_state_call(c_state, q_bf, k_bf, wlv_bf, colt):
    row_spec = lambda w: pl.BlockSpec((S_SB * S_BLK, w), lambda i: (i, 0))
    c_spec = pl.BlockSpec((1, S_SB * 8, N_HEADS, HEAD_DIM, HEAD_DIM), lambda i: (0, i, 0, 0, 0))
    return pl.pallas_call(
        _sample_state_body,
        grid=(S_RB // S_SB,),
        in_specs=[c_spec, row_spec(MIX), row_spec(MIX), row_spec(MIX), row_spec(128)],
        out_specs=(row_spec(MIX), c_spec),
        out_shape=(jax.ShapeDtypeStruct((ROWS, MIX), F32), jax.ShapeDtypeStruct(c_state.shape, F32)),
        compiler_params=pltpu.CompilerParams(dimension_semantics=("arbitrary",), vmem_limit_bytes=VMEM_LIMIT),
        name="sample_state",
    )(c_state, q_bf, k_bf, wlv_bf, colt)


def _sample_back_body(x_ref, p_ref, num_ref, dec_ref, den_ref, qc_ref, og_ref, zg_ref, xc_ref, yb_ref,
                      wout_ref, wple_ref, wpg_ref, lnhead_ref, skip_ref, lnple_ref, lnfin_ref, y_ref):
    ha = (num_ref[...] + dec_ref[...] * qc_ref[...]) / den_ref[...] * og_ref[...]
    ha = jnp.concatenate([_head_norm(ha[:, h * HEAD_DIM:(h + 1) * HEAD_DIM]) for h in range(N_HEADS)], axis=-1)
    ha = (ha * lnhead_ref[...] + skip_ref[...] * xc_ref[...]) * zg_ref[...]
    y_ref[...] = _merge_out(x_ref[...], p_ref[...], ha, yb_ref[...], wout_ref, wple_ref, wpg_ref,
                            lnple_ref, lnfin_ref)


def _sample_back_call(x_tm, p_tm, num, dec, den, qc, og, zg, xc, yb, weights, vecs):
    (_, _, _, _, _, _, _, _, wout, wple, wpg) = weights
    (_, _, _, _, _, lnhead, skip, _, _, lnple, lnfin) = vecs
    return pl.pallas_call(
        _sample_back_body, out_shape=jax.ShapeDtypeStruct((ROWS, D_MODEL), F32),
        compiler_params=pltpu.CompilerParams(vmem_limit_bytes=VMEM_LIMIT),
        name="sample_back",
    )(x_tm, p_tm, num, dec, den, qc, og, zg, xc, yb, wout, wple, wpg, lnhead, skip, lnple, lnfin)


def _to_blocks(a, n_t):
    nb, _, w = a.shape
    return jnp.transpose(a.reshape(nb // 8, 8, n_t, w), (0, 2, 1, 3)).reshape(nb * n_t, w)


def _from_blocks(a, n_t):
    rows, w = a.shape
    nb = rows // n_t
    return jnp.transpose(a.reshape(nb // 8, n_t, 8, w), (0, 2, 1, 3)).reshape(nb, n_t, w)


def _sample_path(x_sample, p_sample, c_state, n_state, m_state, conv_state, sre_state, sim_state,
                 weights, vecs, are8, aim8):
    nb, n_t, _ = x_sample.shape
    assert nb == S_RB * 8 and n_t == S_T
    x_tm = _to_blocks(x_sample, S_T)
    p_tm = _to_blocks(p_sample, S_T)
    conv_tm = _to_blocks(conv_state, CONV_W - 1)
    m_rows = jnp.broadcast_to(jnp.transpose(m_state).reshape(N_HEADS, S_RB, 1, 8), (N_HEADS, S_RB, S_T, 8))
    mrow_in = jnp.pad(m_rows.reshape(N_HEADS, ROWS), ((0, 8 - N_HEADS), (0, 0)))
    (q_bf, k_bf, wlv_bf, colt, num, dec, den, og, zg, xc, yb, n_new, mrow, conv_new, sre, sim) = _sample_front_call(
        x_tm, weights, vecs, are8, aim8, n_state.reshape(nb, MIX), mrow_in, conv_tm,
        sre_state.reshape(nb, N_STATE), sim_state.reshape(nb, N_STATE))
    qc, c_new = _sample_state_call(c_state, q_bf, k_bf, wlv_bf, colt)
    y_tm = _sample_back_call(x_tm, p_tm, num, dec, den, qc, og, zg, xc, yb, weights, vecs)
    m_new = jnp.transpose(mrow[0:N_HEADS].reshape(N_HEADS, S_RB, S_T, 8)[:, :, 0, :].reshape(N_HEADS, nb))
    return (_from_blocks(y_tm, S_T), c_new, n_new.reshape(1, nb, N_HEADS, HEAD_DIM), m_new[None],
            _from_blocks(conv_new, CONV_W - 1)[None], sre.reshape(1, nb, N_GROUPS, S5_STATE),
            sim.reshape(1, nb, N_GROUPS, S5_STATE))


IN_SPLITS = (0, MIX, 2 * MIX, 3 * MIX + 2 * N_HEADS, 4 * MIX + 2 * N_HEADS)
GATE_ROW = 3 * MIX


def _weight_prep_body(wint_ref, wq_ref, wk_ref, wv_ref, wglu_ref, wout_ref, wple_ref, wpg_ref,
                      wbig_ref, wgate_ref, wqk_ref, wvo_ref, wgluo_ref, wouto_ref, wpleo_ref, wpgo_ref):
    for s, r0 in enumerate(IN_SPLITS):
        wbig_ref[:, s * MIX:(s + 1) * MIX] = wint_ref[r0:r0 + MIX, :].T.astype(BF16)
    g8 = wint_ref[GATE_ROW:GATE_ROW + 8, :]
    row = lax.broadcasted_iota(jnp.int32, g8.shape, 0)
    zero = jnp.zeros_like(g8)
    top = jnp.where(row < N_HEADS, g8, zero)
    bot = jnp.where(row < N_HEADS, pltpu.roll(g8, N_HEADS, axis=0), zero)
    wgate_ref[...] = jnp.concatenate([top, bot], axis=0).astype(BF16)
    for h in range(N_HEADS):
        wqk_ref[h, :, 0:HEAD_DIM] = wq_ref[0, h].astype(BF16)
        wqk_ref[h, :, HEAD_DIM:2 * HEAD_DIM] = wk_ref[0, h].astype(BF16)
    wvo_ref[...] = wv_ref[0].astype(BF16)
    wgluo_ref[...] = wglu_ref[0].astype(BF16)
    wouto_ref[...] = wout_ref[0].astype(BF16)
    wpleo_ref[...] = wple_ref[0].astype(BF16)
    wpgo_ref[...] = wpg_ref[0].astype(BF16)


def _weight_prep_call(w_in_t, w_q, w_k, w_v, w_glu, w_out, w_ple, w_pg):
    bf = lambda *s: jax.ShapeDtypeStruct(s, BF16)
    return pl.pallas_call(
        _weight_prep_body,
        out_shape=(bf(D_MODEL, 5 * MIX), bf(16, D_MODEL), bf(N_HEADS, HEAD_DIM, 2 * HEAD_DIM),
                   bf(N_HEADS, HEAD_DIM, HEAD_DIM), bf(MIX, MIX), bf(2 * MIX, D_MODEL), bf(D_PLE, D_MODEL),
                   bf(D_MODEL, D_MODEL)),
        compiler_params=pltpu.CompilerParams(vmem_limit_bytes=VMEM_LIMIT),
        name="weight_prep",
    )(w_in_t, w_q, w_k, w_v, w_glu, w_out, w_ple, w_pg)


def _layout_params(ln_mix, w_in, b_igate, b_fgate, conv_w, conv_b, w_q, w_k, w_v, ln_head, skip_a,
                   s5_B_re, s5_B_im, s5_C_re, s5_C_im, s5_D, w_glu, b_glu, w_out, w_ple, ln_ple,
                   w_ple_gate, ln_final):
    wbig, wgate, wqk, wv, wglu_bf, wout_bf, wple_bf, wpg_bf = _weight_prep_call(
        jnp.transpose(w_in[0]), w_q, w_k, w_v, w_glu, w_out, w_ple, w_ple_gate)
    eye = jnp.eye(16, dtype=F32)

    def c_blocks(cmat):
        cr = cmat.reshape(2, 16, S5_GROUP, S5_STATE)
        return jnp.einsum('ab,hbcp->hapbc', eye, cr).reshape(2, 16 * S5_STATE, 16 * S5_GROUP).astype(BF16)

    def b_tiled(bmat):
        bt = jnp.transpose(bmat.reshape(2, 16, S5_STATE, S5_GROUP), (0, 1, 3, 2)).reshape(2, 256, 1, S5_STATE)
        return jnp.broadcast_to(bt, (2, 256, 16, S5_STATE)).reshape(2, 256, 16 * S5_STATE)

    weights = [wbig, wgate, wqk, wv, None, c_blocks(s5_C_re[0]), c_blocks(s5_C_im[0]),
               wglu_bf, wout_bf, wple_bf, wpg_bf]
    pad8 = lambda v: jnp.pad(v[0][:, None], ((0, 8 - N_HEADS), (0, 0)))
    vecs = [ln_mix[0][None, :], conv_w[0], conv_b[0][None, :], pad8(b_igate), pad8(b_fgate),
            ln_head[0][None, :], skip_a[0][None, :], s5_D[0][None, :], b_glu[0][None, :],
            ln_ple[0][None, :], ln_final[None, :]]
    return weights, vecs, b_tiled(s5_B_re[0]), b_tiled(s5_B_im[0])


def kernel(x_prompt, x_sample, p_prompt, p_sample, state_mlstm_C, state_mlstm_n, state_mlstm_m, state_conv, state_s5_re, state_s5_im, ln_mix, w_in, b_igate, b_fgate, conv_w, conv_b, w_q, w_k, w_v, ln_head, skip_a, s5_lam_re, s5_lam_im, s5_log_dt, s5_B_re, s5_B_im, s5_C_re, s5_C_im, s5_D, w_glu, b_glu, w_out, w_ple, ln_ple, w_ple_gate, ln_final):
    weights, vecs, bt_re, bt_im = _layout_params(
        ln_mix, w_in, b_igate, b_fgate, conv_w, conv_b, w_q, w_k, w_v, ln_head, skip_a,
        s5_B_re, s5_B_im, s5_C_re, s5_C_im, s5_D, w_glu, b_glu, w_out, w_ple, ln_ple, w_ple_gate, ln_final)
    are8, aim8, wb = _prep_call(s5_lam_re.reshape(1, N_STATE), s5_lam_im.reshape(1, N_STATE),
                                s5_log_dt.reshape(1, N_STATE), bt_re, bt_im)
    weights[4] = wb

    nbp = x_prompt.shape[0]
    y_p, c_p, n_p, mrow_p, conv_p, sre_p, sim_p = _prompt_call(x_prompt, p_prompt[0], weights, vecs, are8, aim8)
    pn = n_p.reshape(1, nbp, N_HEADS, HEAD_DIM)
    pm = jnp.transpose(mrow_p[0:N_HEADS, ::P_T])[None]
    pre = sre_p.reshape(1, nbp, N_GROUPS, S5_STATE)
    pim = sim_p.reshape(1, nbp, N_GROUPS, S5_STATE)

    y_s, c_s, n_s, m_s, conv_s, sre_s, sim_s = _sample_path(
        x_sample, p_sample[0], state_mlstm_C, state_mlstm_n[0], state_mlstm_m[0], state_conv[0],
        state_s5_re[0], state_s5_im[0], weights, vecs, are8, aim8)
    return (y_p, y_s, c_p, pn, pm, conv_p, pre, pim, c_s, n_s, m_s, conv_s, sre_s, sim_s)
```

```python
import functools
import math

import jax
import jax.numpy as jnp
from jax import lax
from jax.experimental import pallas as pl
from jax.experimental.pallas import tpu as pltpu

F32 = jnp.float32
BF16 = jnp.bfloat16

D_MODEL = 1024
D_PLE = 256
MIX = 512
N_HEADS = 4
HEAD_DIM = 128
N_GROUPS = 32
S5_GROUP = 16
S5_STATE = 64
N_STATE = N_GROUPS * S5_STATE
CONV_W = 4
EPS = 1e-6
K_SCALE = HEAD_DIM ** -0.5
ROWS = 512
GROUP_ROWS = 256
VMEM_LIMIT = 60 * 1024 * 1024


def _mm(a, w):
    return jnp.dot(a, w, preferred_element_type=F32)


def _rms(x, g):
    ms = jnp.mean(x * x, axis=-1, keepdims=True)
    return x * lax.rsqrt(ms + EPS) * g


def _sigmoid(x):
    return 0.5 * jnp.tanh(0.5 * x) + 0.5


def _silu(x):
    h = 0.5 * x
    return h + h * jnp.tanh(h)


def _gelu_tanh(x):
    c = math.sqrt(2.0 / math.pi)
    return 0.5 * x * (1.0 + jnp.tanh(c * (x + 0.044715 * (x * x * x))))


def _log_sigmoid(x):
    return jnp.minimum(x, 0.0) - jnp.log1p(jnp.exp(-jnp.abs(x)))


SCAN_RADIX = 8


def _seg_scan(x, tidx, n_t, t_stride, op, fill):
    span = 1
    while span < n_t:
        acc = x
        for j in range(1, SCAN_RADIX):
            if j * span >= n_t:
                break
            prev = pltpu.roll(x, j * span * t_stride, axis=1)
            acc = op(acc, jnp.where(tidx >= j * span, prev, fill))
        x = acc
        span *= SCAN_RADIX
    return x


def _bcast_last(x, tidx, n_t, t_stride):
    y = jnp.where(tidx == n_t - 1, x, 0.0)
    span = 1
    while span < n_t:
        acc = y
        for j in range(1, SCAN_RADIX):
            if j * span >= n_t:
                break
            nxt = pltpu.roll(y, ROWS - j * span * t_stride, axis=1)
            acc = acc + jnp.where(tidx + j * span <= n_t - 1, nxt, 0.0)
        y = acc
        span *= SCAN_RADIX
    return y


def _gate_rows(a, wg_ref, bi8, bf8, mprev, n_t, t_stride):
    gt = lax.dot_general(wg_ref[...], a, (((1,), (1,)), ((), ())), preferred_element_type=F32)
    lane = lax.broadcasted_iota(jnp.int32, (8, ROWS), 1)
    tidx = (lane // t_stride) % n_t
    i8 = gt[0:8] + bi8
    logf = _log_sigmoid(gt[8:16] + bf8)
    b = _seg_scan(logf, tidx, n_t, t_stride, jnp.add, 0.0)
    c = i8 - b
    cm = _seg_scan(c, tidx, n_t, t_stride, jnp.maximum, -jnp.inf)
    big_m = jnp.maximum(mprev, cm)
    mt = b + big_m
    m_last = _bcast_last(big_m, tidx, n_t, t_stride)
    mnew = _bcast_last(mt, tidx, n_t, t_stride)
    stack = jnp.concatenate(
        [-big_m, jnp.exp(mprev - big_m), jnp.exp(-mt), jnp.exp(c - m_last), jnp.exp(mprev - m_last),
         jnp.zeros((128 - 40, ROWS), F32)], axis=0)
    return c, stack.T, mnew


COL_NEGM, COL_DECAY, COL_EM, COL_WL, COL_DL = 0, 8, 16, 24, 32


def _intra(qg, kg, vg, col_a, row_b, mask):
    s = lax.dot_general(qg, kg, (((1,), (1,)), ((), ())), preferred_element_type=F32)
    p = s * jnp.exp(jnp.where(mask, col_a + row_b, -jnp.inf))
    rs = jnp.sum(p, axis=-1, keepdims=True)
    return _mm(p.astype(BF16), vg), rs


def _head_norm(x):
    mu = jnp.mean(x, axis=-1, keepdims=True)
    xc = x - mu
    var = jnp.mean(xc * xc, axis=-1, keepdims=True)
    return xc * lax.rsqrt(var + EPS)


S5_HALF = N_STATE // 2


def _s5_scan(hf, bure_s, buim_s, xsre_s, xsim_s, are_ref, aim_ref, sre_ref, sim_ref, row_base, st_rows, n_t):
    chunk = 512
    assert n_t % 2 == 0 and row_base % 16 == 0
    for ch in range(S5_HALF // chunk):
        ls = slice(ch * chunk, (ch + 1) * chunk)
        gl = slice(hf * S5_HALF + ch * chunk, hf * S5_HALF + (ch + 1) * chunk)
        ar = are_ref[:, gl]
        ai = aim_ref[:, gl]
        xr = sre_ref[st_rows, gl]
        xi = sim_ref[st_rows, gl]
        for t in range(0, n_t, 2):
            r0 = row_base + t * 8
            xr1, xi1 = (ar * xr - ai * xi + bure_s[r0:r0 + 8, ls], ar * xi + ai * xr + buim_s[r0:r0 + 8, ls])
            xr, xi = (ar * xr1 - ai * xi1 + bure_s[r0 + 8:r0 + 16, ls],
                      ar * xi1 + ai * xr1 + buim_s[r0 + 8:r0 + 16, ls])
            xsre_s[r0:r0 + 16, ls] = jnp.concatenate([xr1, xr], axis=0).astype(BF16)
            xsim_s[r0:r0 + 16, ls] = jnp.concatenate([xi1, xi], axis=0).astype(BF16)
        sre_ref[st_rows, gl] = xr
        sim_ref[st_rows, gl] = xi


def _s5_branch(u_tm, wb_ref, wcre_ref, wcim_ref, bure_s, buim_s, xsre_s, xsim_s, are_ref, aim_ref,
               sre_ref, sim_ref, blocks):
    views = [[r.at[hf] for r in (bure_s, buim_s, xsre_s, xsim_s)] for hf in range(2)]
    for hf in range(2):
        res = _mm(u_tm[:, hf * 256:(hf + 1) * 256].astype(BF16), wb_ref[hf])
        views[hf][0][...] = res[:, :S5_HALF]
        views[hf][1][...] = res[:, S5_HALF:]
    for hf in range(2):
        for row_base, st_rows, n_t in blocks:
            _s5_scan(hf, *views[hf], are_ref, aim_ref, sre_ref, sim_ref, row_base, st_rows, n_t)
    return [_mm(views[hf][2][...], wcre_ref[hf]) - _mm(views[hf][3][...], wcim_ref[hf]) for hf in range(2)]


def _s5_post(y_s5, u, a, wbig_ref, wglu_ref, s5d_ref, bglu_ref):
    yb = _gelu_tanh(y_s5 + s5d_ref[...] * u)
    yb = yb * _sigmoid(_mm(yb.astype(BF16), wglu_ref[...]) + bglu_ref[...])
    return yb * _silu(_mm(a, wbig_ref[:, 2048:2560]))


OUT_BLOCK = 256


def _merge_out(x, p, h_a, y_b, wout_ref, wple_ref, wpg_ref, lnple_ref, lnfin_ref):
    h1 = x + _mm(h_a.astype(BF16), wout_ref[0:MIX, :]) + _mm(y_b.astype(BF16), wout_ref[MIX:2 * MIX, :])
    e_half = 0.5 * _rms(_mm(p.astype(BF16), wple_ref[...]), lnple_ref[...])
    h1_bf = h1.astype(BF16)
    base = h1 + e_half
    blocks = []
    ssq = jnp.zeros((h1.shape[0], 1), F32)
    for nb in range(D_MODEL // OUT_BLOCK):
        cs = slice(nb * OUT_BLOCK, (nb + 1) * OUT_BLOCK)
        h2 = base[:, cs] + e_half[:, cs] * jnp.tanh(0.5 * _mm(h1_bf, wpg_ref[:, cs]))
        ssq = ssq + jnp.sum(h2 * h2, axis=-1, keepdims=True)
        blocks.append(h2)
    scale = lax.rsqrt(ssq * (1.0 / D_MODEL) + EPS)
    return jnp.concatenate(blocks, axis=-1) * scale * lnfin_ref[...]


def _prep_body(lam_ref, btr_ref, bti_ref, ctr_ref, cti_ref, are_ref, aim_ref, wb_ref, wcre_ref, wcim_ref):
    lr = jnp.minimum(lam_ref[0:1, :], -1e-4)
    li = lam_ref[1:2, :]
    dt = jnp.exp(lam_ref[2:3, :])
    mag = jnp.exp(lr * dt)
    a_re = mag * jnp.cos(li * dt)
    a_im = mag * jnp.sin(li * dt)
    den = lr * lr + li * li
    xr = a_re - 1.0
    g_re = (xr * lr + a_im * li) / den
    g_im = (a_im * lr - xr * li) / den
    are_ref[...] = jnp.broadcast_to(a_re, (8, N_STATE))
    aim_ref[...] = jnp.broadcast_to(a_im, (8, N_STATE))
    half = N_STATE // 2
    row_s = lax.broadcasted_iota(jnp.int32, (half, 256), 0) // S5_STATE
    col_c = lax.broadcasted_iota(jnp.int32, (half, 256), 1) // S5_GROUP
    row_c = lax.broadcasted_iota(jnp.int32, (256, half), 0) // S5_GROUP
    col_s = lax.broadcasted_iota(jnp.int32, (256, half), 1) // S5_STATE
    for hf in range(2):
        gr = g_re[:, hf * half:(hf + 1) * half]
        gi = g_im[:, hf * half:(hf + 1) * half]
        br = jnp.where(row_s == col_c, btr_ref[hf], 0.0).T
        bi = jnp.where(row_s == col_c, bti_ref[hf], 0.0).T
        wb_ref[hf, :, 0:half] = (gr * br - gi * bi).astype(BF16)
        wb_ref[hf, :, half:2 * half] = (gr * bi + gi * br).astype(BF16)
        wcre_ref[hf] = jnp.where(row_c == col_s, ctr_ref[hf], 0.0).T.astype(BF16)
        wcim_ref[hf] = jnp.where(row_c == col_s, cti_ref[hf], 0.0).T.astype(BF16)


def _prep_call(lam, bt_re, bt_im, ct_re, ct_im):
    half = N_STATE // 2
    return pl.pallas_call(
        _prep_body,
        out_shape=(jax.ShapeDtypeStruct((8, N_STATE), F32), jax.ShapeDtypeStruct((8, N_STATE), F32),
                   jax.ShapeDtypeStruct((2, 256, N_STATE), BF16),
                   jax.ShapeDtypeStruct((2, half, 256), BF16), jax.ShapeDtypeStruct((2, half, 256), BF16)),
        compiler_params=pltpu.CompilerParams(vmem_limit_bytes=VMEM_LIMIT),
        name="s5_prep",
    )(lam, bt_re, bt_im, ct_re, ct_im)


P_T = 64
P_NB = 8


P_SUB = 1


def _prompt_body(*refs):
    n_in = 26
    c_ref, n_ref, mrow_ref = refs[n_in + 1:n_in + 4]
    sre_ref, sim_ref, ext_s = refs[n_in + 5:n_in + 8]
    assert len(refs) == n_in + 7 + 9

    @pl.when(pl.program_id(0) == 0)
    def _init():
        c_ref[...] = jnp.zeros_like(c_ref)
        n_ref[...] = jnp.zeros_like(n_ref)
        mrow_ref[...] = jnp.zeros_like(mrow_ref)
        sre_ref[...] = jnp.zeros_like(sre_ref)
        sim_ref[...] = jnp.zeros_like(sim_ref)
        ext_s[:, 0:8, :] = jnp.zeros((P_NB, 8, MIX), F32)

    for sub in range(P_SUB):
        _prompt_tile(slice(sub * P_T, (sub + 1) * P_T), *refs)


def _prompt_tile(ts, x_ref, p_ref, wbig_ref, wgate_ref, wqk_ref, wv_ref, wb_ref, wcre_ref, wcim_ref,
                 wglu_ref, wout_ref, wple_ref, wpg_ref, lnmix_ref, convw_ref, convb_ref, bi8_ref, bf8_ref,
                 lnhead_ref, skip_ref, s5d_ref, bglu_ref, lnple_ref, lnfin_ref, are_ref, aim_ref,
                 y_ref, c_ref, n_ref, mrow_ref, conv_ref, sre_ref, sim_ref,
                 ext_s, ha_s, utm_s, ytm_s, ybm_s, bure_s, buim_s, xsre_s, xsim_s):
    x = x_ref[:, ts, :].reshape(ROWS, D_MODEL)
    a = _rms(x, lnmix_ref[...]).astype(BF16)

    c8, colt, mnew = _gate_rows(a, wgate_ref, bi8_ref[...], bf8_ref[...], mrow_ref[...], P_T, 1)
    mrow_ref[...] = mnew

    xm = _mm(a, wbig_ref[:, 0:512])
    ext_s[:, 8:8 + P_T, :] = xm.reshape(P_NB, P_T, MIX)
    yc = convb_ref[...]
    for j in range(CONV_W):
        yc = yc + ext_s[:, 5 + j:5 + j + P_T, :].reshape(ROWS, MIX) * convw_ref[j:j + 1, :]
    for j in range(CONV_W - 1):
        conv_ref[j] = ext_s[:, 5 + P_T + j, :]
    ext_s[:, 5:8, :] = ext_s[:, 5 + P_T:8 + P_T, :]
    xc = _silu(yc)
    xc_bf = xc.astype(BF16)
    xm_bf = xm.astype(BF16)

    r = lax.broadcasted_iota(jnp.int32, (GROUP_ROWS, GROUP_ROWS), 0)
    cc = lax.broadcasted_iota(jnp.int32, (GROUP_ROWS, GROUP_ROWS), 1)
    mask = ((r // P_T) == (cc // P_T)) & (cc <= r)

    for h in range(N_HEADS):
        hs = slice(h * HEAD_DIM, (h + 1) * HEAD_DIM)
        qk = _mm(xc_bf[:, hs], wqk_ref[h])
        q_f = qk[:, :HEAD_DIM]
        k_f = qk[:, HEAD_DIM:] * K_SCALE
        v_f = _mm(xm_bf[:, hs], wv_ref[h])
        q_bf = q_f.astype(BF16)
        k_bf = k_f.astype(BF16)
        v_bf = v_f.astype(BF16)
        for g in range(ROWS // GROUP_ROWS):
            gs = slice(g * GROUP_ROWS, (g + 1) * GROUP_ROWS)
            decay = colt[gs, COL_DECAY + h:COL_DECAY + h + 1]
            wl = colt[gs, COL_WL + h:COL_WL + h + 1]
            dl = colt[gs, COL_DL + h:COL_DL + h + 1]
            num, rs = _intra(q_bf[gs], k_bf[gs], v_bf[gs], colt[gs, COL_NEGM + h:COL_NEGM + h + 1],
                             c8[h:h + 1, gs], mask)
            wlv = (wl * v_f[gs]).astype(BF16)
            wlk = wl * k_f[gs]
            qcs, qns = [], []
            for bb in range(GROUP_ROWS // P_T):
                b = g * (GROUP_ROWS // P_T) + bb
                rb = slice(g * GROUP_ROWS + bb * P_T, g * GROUP_ROWS + (bb + 1) * P_T)
                rl = slice(bb * P_T, (bb + 1) * P_T)
                c_old = c_ref[0, b, h]
                n_old = n_ref[b:b + 1, hs]
                qcs.append(_mm(q_bf[rb], c_old.astype(BF16)))
                qns.append(jnp.sum(q_f[rb] * n_old, axis=-1, keepdims=True))
                upd = lax.dot_general(k_bf[rb], wlv[rl], (((0,), (0,)), ((), ())), preferred_element_type=F32)
                dlb = dl[rl]
                c_ref[0, b, h] = jnp.concatenate([dlb, dlb], axis=0) * c_old + upd
                n_ref[b:b + 1, hs] = dlb[0:1] * n_old + jnp.sum(wlk[rl], axis=0, keepdims=True)
            num = num + decay * jnp.concatenate(qcs, axis=0)
            qn = rs + decay * jnp.concatenate(qns, axis=0)
            den = jnp.maximum(jnp.abs(qn), colt[gs, COL_EM + h:COL_EM + h + 1])
            ha_s[gs, hs] = num / den

    ha = ha_s[...] * _sigmoid(_mm(a, wbig_ref[:, 1024:1536]))
    ha = jnp.concatenate([_head_norm(ha[:, h * HEAD_DIM:(h + 1) * HEAD_DIM]) for h in range(N_HEADS)], axis=-1)
    ha = (ha * lnhead_ref[...] + skip_ref[...] * xc) * _silu(_mm(a, wbig_ref[:, 512:1024]))

    u = _mm(a, wbig_ref[:, 1536:2048])
    for b in range(P_NB):
        for j in range(P_T // 8):
            for lc in range(MIX // 128):
                utm_s[lc, pl.ds(j * 64 + b, 8, stride=8), :] = (
                    u[b * P_T + j * 8:b * P_T + j * 8 + 8, lc * 128:(lc + 1) * 128])
    u_tm = jnp.concatenate([utm_s[lc] for lc in range(MIX // 128)], axis=-1)
    y_halves = _s5_branch(u_tm, wb_ref, wcre_ref, wcim_ref, bure_s, buim_s, xsre_s, xsim_s, are_ref, aim_ref,
                          sre_ref, sim_ref, [(0, slice(0, 8), P_T)])
    for hf, y_half in enumerate(y_halves):
        ytm_s[2 * hf] = y_half[:, 0:128]
        ytm_s[2 * hf + 1] = y_half[:, 128:256]
    for b in range(P_NB):
        for j in range(P_T // 8):
            for lc in range(MIX // 128):
                ybm_s[b * P_T + j * 8:b * P_T + j * 8 + 8, lc * 128:(lc + 1) * 128] = (
                    ytm_s[lc, pl.ds(j * 64 + b, 8, stride=8), :])
    yb = _s5_post(ybm_s[...], u, a, wbig_ref, wglu_ref, s5d_ref, bglu_ref)

    p = p_ref[:, ts, :].reshape(ROWS, D_PLE)
    y = _merge_out(x, p, ha, yb, wout_ref, wple_ref, wpg_ref, lnple_ref, lnfin_ref)
    y_ref[:, ts, :] = y.reshape(P_NB, P_T, D_MODEL)


def _const_spec(shape):
    nd = len(shape)
    return pl.BlockSpec(shape, lambda i, nd=nd: (0,) * nd, pipeline_mode=pl.Buffered(1))


def _prompt_call(x, p, weights, vecs, are8, aim8):
    nb, seq, _ = x.shape
    blk_t = P_T * P_SUB
    assert nb == P_NB and seq % blk_t == 0
    n_steps = seq // blk_t
    w_specs = [_const_spec(w.shape) for w in weights]
    v_specs = [_const_spec(v.shape) for v in vecs]
    in_specs = ([pl.BlockSpec((P_NB, blk_t, D_MODEL), lambda i: (0, i, 0)),
                 pl.BlockSpec((P_NB, blk_t, D_PLE), lambda i: (0, i, 0))]
                + w_specs + v_specs + [_const_spec((8, N_STATE)), _const_spec((8, N_STATE))])
    out_shape = (
        jax.ShapeDtypeStruct((nb, seq, D_MODEL), F32),
        jax.ShapeDtypeStruct((1, nb, N_HEADS, HEAD_DIM, HEAD_DIM), F32),
        jax.ShapeDtypeStruct((nb, MIX), F32),
        jax.ShapeDtypeStruct((8, ROWS), F32),
        jax.ShapeDtypeStruct((CONV_W - 1, nb, MIX), F32),
        jax.ShapeDtypeStruct((8, N_STATE), F32),
        jax.ShapeDtypeStruct((8, N_STATE), F32),
    )
    out_specs = (
        pl.BlockSpec((P_NB, blk_t, D_MODEL), lambda i: (0, i, 0)),
        pl.BlockSpec((1, nb, N_HEADS, HEAD_DIM, HEAD_DIM), lambda i: (0, 0, 0, 0, 0)),
        pl.BlockSpec((nb, MIX), lambda i: (0, 0)),
        pl.BlockSpec((8, ROWS), lambda i: (0, 0)),
        pl.BlockSpec((CONV_W - 1, nb, MIX), lambda i: (0, 0, 0)),
        pl.BlockSpec((8, N_STATE), lambda i: (0, 0)),
        pl.BlockSpec((8, N_STATE), lambda i: (0, 0)),
    )
    scratch = [
        pltpu.VMEM((P_NB, P_T + 8, MIX), F32),
        pltpu.VMEM((ROWS, MIX), F32),
        pltpu.VMEM((MIX // 128, ROWS, 128), F32),
        pltpu.VMEM((MIX // 128, ROWS, 128), F32),
        pltpu.VMEM((ROWS, MIX), F32),
        pltpu.VMEM((2, ROWS, S5_HALF), F32),
        pltpu.VMEM((2, ROWS, S5_HALF), F32),
        pltpu.VMEM((2, ROWS, S5_HALF), BF16),
        pltpu.VMEM((2, ROWS, S5_HALF), BF16),
    ]
    return pl.pallas_call(
        _prompt_body,
        grid=(n_steps,),
        in_specs=in_specs,
        out_specs=out_specs,
        out_shape=out_shape,
        scratch_shapes=scratch,
        compiler_params=pltpu.CompilerParams(dimension_semantics=("arbitrary",), vmem_limit_bytes=VMEM_LIMIT),
        name="prompt_layer",
    )(x, p, *weights, *vecs, are8, aim8)


S_T = 4
S_RB = ROWS // (8 * S_T)
S_BLK = 8 * S_T


def _sample_front_body(x_ref, wbig_ref, wgate_ref, wqk_ref, wv_ref, wb_ref, wcre_ref, wcim_ref, wglu_ref,
                       lnmix_ref, convw_ref, convb_ref, bi8_ref, bf8_ref, s5d_ref, bglu_ref, are_ref, aim_ref,
                       n_in_ref, mrow_in_ref, conv_in_ref, sre_in_ref, sim_in_ref,
                       q_ref, k_ref, wlv_ref, colt_ref, num_ref, dec_ref, den_ref, og_ref, zg_ref, xc_ref, yb_ref,
                       n_ref, mrow_ref, conv_ref, sre_out_ref, sim_out_ref,
                       ext_s, bure_s, buim_s, xsre_s, xsim_s, xtm_s, sre_ref, sim_ref):
    x = _rows_to_blocks(x_ref, xtm_s)
    a = _rms(x, lnmix_ref[...]).astype(BF16)

    xm = _mm(a, wbig_ref[:, 0:512])
    ext_rows = (CONV_W - 1 + S_T) * 8
    ycs = []
    for rb in range(S_RB):
        seqs = slice(rb * 8, (rb + 1) * 8)
        for j in range(CONV_W - 1):
            ext_s[rb * ext_rows + 8 * j:rb * ext_rows + 8 * j + 8, :] = conv_in_ref[j, seqs, :]
        ext_s[rb * ext_rows + 24:(rb + 1) * ext_rows, :] = xm[rb * S_BLK:(rb + 1) * S_BLK, :]
        yc = convb_ref[...]
        for j in range(CONV_W):
            yc = yc + ext_s[rb * ext_rows + 8 * j:rb * ext_rows + 8 * j + S_BLK, :] * convw_ref[j:j + 1, :]
        ycs.append(yc)
        for j in range(CONV_W - 1):
            conv_ref[j, seqs, :] = ext_s[rb * ext_rows + S_BLK + 8 * j:rb * ext_rows + S_BLK + 8 * j + 8, :]
    xc = _silu(jnp.concatenate(ycs, axis=0))
    xc_ref[...] = xc
    xc_bf = xc.astype(BF16)
    xm_bf = xm.astype(BF16)

    mprev = mrow_in_ref[...]
    c8, colt, mnew = _gate_rows(a, wgate_ref, bi8_ref[...], bf8_ref[...], mprev, S_T, 8)
    mrow_ref[...] = mnew
    colt_ref[...] = colt

    r = lax.broadcasted_iota(jnp.int32, (GROUP_ROWS, GROUP_ROWS), 0)
    cc = lax.broadcasted_iota(jnp.int32, (GROUP_ROWS, GROUP_ROWS), 1)
    same_seq = ((r // S_BLK) == (cc // S_BLK)) & ((r % 8) == (cc % 8))
    mask = same_seq & (((cc // 8) % S_T) <= ((r // 8) % S_T))

    nb = S_RB * 8
    for h in range(N_HEADS):
        hs = slice(h * HEAD_DIM, (h + 1) * HEAD_DIM)
        qk = _mm(xc_bf[:, hs], wqk_ref[h])
        q_f = qk[:, :HEAD_DIM]
        k_f = qk[:, HEAD_DIM:] * K_SCALE
        v_f = _mm(xm_bf[:, hs], wv_ref[h])
        q_bf = q_f.astype(BF16)
        k_bf = k_f.astype(BF16)
        v_bf = v_f.astype(BF16)
        decay = colt[:, COL_DECAY + h:COL_DECAY + h + 1]
        wl = colt[:, COL_WL + h:COL_WL + h + 1]
        dl = colt[:, COL_DL + h:COL_DL + h + 1]
        nums, rss = [], []
        for g in range(ROWS // GROUP_ROWS):
            gs = slice(g * GROUP_ROWS, (g + 1) * GROUP_ROWS)
            num, rs = _intra(q_bf[gs], k_bf[gs], v_bf[gs], colt[gs, COL_NEGM + h:COL_NEGM + h + 1],
                             c8[h:h + 1, gs], mask)
            nums.append(num)
            rss.append(rs)
        q_ref[:, hs] = q_bf
        k_ref[:, hs] = k_bf
        wlv_ref[:, hs] = (wl * v_f).astype(BF16)
        num_ref[:, hs] = jnp.concatenate(nums, axis=0)
        n_old = n_in_ref[:, h, :]
        n_rows = jnp.broadcast_to(n_old.reshape(S_RB, 1, 8, HEAD_DIM), (S_RB, S_T, 8, HEAD_DIM)).reshape(ROWS, HEAD_DIM)
        qn = jnp.concatenate(rss, axis=0) + decay * jnp.sum(q_f * n_rows, axis=-1, keepdims=True)
        den = jnp.maximum(jnp.abs(qn), colt[:, COL_EM + h:COL_EM + h + 1])
        dec_ref[:, hs] = jnp.broadcast_to(decay, (ROWS, HEAD_DIM))
        den_ref[:, hs] = jnp.broadcast_to(den, (ROWS, HEAD_DIM))
        wlk_sum = jnp.sum((wl * k_f).reshape(S_RB, S_T, 8, HEAD_DIM), axis=1).reshape(nb, HEAD_DIM)
        dl_seq = dl.reshape(S_RB, S_T, 8, 1)[:, 0].reshape(nb, 1)
        n_ref[:, h, :] = dl_seq * n_old + wlk_sum

    og_ref[...] = _sigmoid(_mm(a, wbig_ref[:, 1024:1536]))
    zg_ref[...] = _silu(_mm(a, wbig_ref[:, 512:1024]))

    u = _mm(a, wbig_ref[:, 1536:2048])
    sre_ref[...] = sre_in_ref[...].T
    sim_ref[...] = sim_in_ref[...].T
    blocks = [(rb * S_BLK, slice(rb * 8, (rb + 1) * 8), S_T) for rb in range(S_RB)]
    y_s5 = jnp.concatenate(
        _s5_branch(u, wb_ref, wcre_ref, wcim_ref, bure_s, buim_s, xsre_s, xsim_s, are_ref, aim_ref,
                   sre_ref, sim_ref, blocks), axis=-1)
    sre_out_ref[...] = sre_ref[...].T
    sim_out_ref[...] = sim_ref[...].T
    yb_ref[...] = _s5_post(y_s5, u, a, wbig_ref, wglu_ref, s5d_ref, bglu_ref)


def _rows_to_blocks(src_ref, dst_s):
    for rb in range(S_RB):
        for t in range(S_T):
            dst_s[rb * S_BLK + t * 8:rb * S_BLK + t * 8 + 8, :] = src_ref[rb * 8:(rb + 1) * 8, t, :]
    return dst_s[...]


def _sample_front_call(x3, weights, vecs, are8, aim8, n3, mrow_in, conv_t, sre_t, sim_t):
    (wbig, wgate, wqk, wv, wb, wcre, wcim, wglu, _, _, _) = weights
    (lnmix, convw, convb, bi8, bf8, _, _, s5d, bglu, _, _) = vecs
    nb = S_RB * 8
    f32 = lambda *s: jax.ShapeDtypeStruct(s, F32)
    bf = lambda *s: jax.ShapeDtypeStruct(s, BF16)
    out_shape = (bf(ROWS, MIX), bf(ROWS, MIX), bf(ROWS, MIX), f32(ROWS, 128), f32(ROWS, MIX), f32(ROWS, MIX),
                 f32(ROWS, MIX), f32(ROWS, MIX), f32(ROWS, MIX), f32(ROWS, MIX), f32(ROWS, MIX),
                 f32(nb, N_HEADS, HEAD_DIM), f32(8, ROWS), f32(CONV_W - 1, nb, MIX), f32(N_STATE, nb),
                 f32(N_STATE, nb))
    scratch = [pltpu.VMEM((S_RB * (CONV_W - 1 + S_T) * 8, MIX), F32),
               pltpu.VMEM((2, ROWS, S5_HALF), F32), pltpu.VMEM((2, ROWS, S5_HALF), F32),
               pltpu.VMEM((2, ROWS, S5_HALF), BF16), pltpu.VMEM((2, ROWS, S5_HALF), BF16),
               pltpu.VMEM((ROWS, D_MODEL), F32), pltpu.VMEM((nb, N_STATE), F32), pltpu.VMEM((nb, N_STATE), F32)]
    return pl.pallas_call(
        _sample_front_body, out_shape=out_shape, scratch_shapes=scratch,
        compiler_params=pltpu.CompilerParams(vmem_limit_bytes=VMEM_LIMIT),
        name="sample_front",
    )(x3, wbig, wgate, wqk, wv, wb, wcre, wcim, wglu, lnmix, convw, convb, bi8, bf8, s5d, bglu, are8, aim8,
      n3, mrow_in, conv_t, sre_t, sim_t)


def _sample_state_body(c_ref, q_ref, k_ref, wlv_ref, colt_ref, qc_ref, cnew_ref):
    sub = lax.broadcasted_iota(jnp.int32, (S_BLK, HEAD_DIM), 0) % 8
    for sb in range(S_SB):
        rows = slice(sb * S_BLK, (sb + 1) * S_BLK)
        for h in range(N_HEADS):
            hs = slice(h * HEAD_DIM, (h + 1) * HEAD_DIM)
            q = q_ref[rows, hs]
            k = k_ref[rows, hs]
            wlv = wlv_ref[rows, hs]
            acc = jnp.zeros((S_BLK, HEAD_DIM), F32)
            for b8 in range(8):
                c_old = c_ref[0, sb * 8 + b8, h]
                sel = sub == b8
                acc = jnp.where(sel, _mm(q, c_old.astype(BF16)), acc)
                upd = lax.dot_general(k, jnp.where(sel, wlv, jnp.zeros_like(wlv)), (((0,), (0,)), ((), ())),
                                      preferred_element_type=F32)
                dl = colt_ref[sb * S_BLK + b8:sb * S_BLK + b8 + 1, COL_DL + h:COL_DL + h + 1]
                cnew_ref[0, sb * 8 + b8, h] = dl * c_old + upd
            qc_ref[rows, hs] = acc


S_SB = 2


def _sample_state_call(c_state, q_bf, k_bf, wlv_bf, colt):
    row_spec = lambda w: pl.BlockSpec((S_SB * S_BLK, w), lambda i: (i, 0))
    c_spec = pl.BlockSpec((1, S_SB * 8, N_HEADS, HEAD_DIM, HEAD_DIM), lambda i: (0, i, 0, 0, 0))
    return pl.pallas_call(
        _sample_state_body,
        grid=(S_RB // S_SB,),
        in_specs=[c_spec, row_spec(MIX), row_spec(MIX), row_spec(MIX), row_spec(128)],
        out_specs=(row_spec(MIX), c_spec),
        out_shape=(jax.ShapeDtypeStruct((ROWS, MIX), F32), jax.ShapeDtypeStruct(c_state.shape, F32)),
        compiler_params=pltpu.CompilerParams(dimension_semantics=("arbitrary",), vmem_limit_bytes=VMEM_LIMIT),
        name="sample_state",
    )(c_state, q_bf, k_bf, wlv_bf, colt)


def _sample_back_body(x_ref, p_ref, num_ref, dec_ref, den_ref, qc_ref, og_ref, zg_ref, xc_ref, yb_ref,
                      wout_ref, wple_ref, wpg_ref, lnhead_ref, skip_ref, lnple_ref, lnfin_ref, y_ref,
                      xtm_s, ptm_s):
    ha = (num_ref[...] + dec_ref[...] * qc_ref[...]) / den_ref[...] * og_ref[...]
    ha = jnp.concatenate([_head_norm(ha[:, h * HEAD_DIM:(h + 1) * HEAD_DIM]) for h in range(N_HEADS)], axis=-1)
    ha = (ha * lnhead_ref[...] + skip_ref[...] * xc_ref[...]) * zg_ref[...]
    y = _merge_out(_rows_to_blocks(x_ref, xtm_s), _rows_to_blocks(p_ref, ptm_s), ha, yb_ref[...],
                   wout_ref, wple_ref, wpg_ref, lnple_ref, lnfin_ref)
    for rb in range(S_RB):
        for t in range(S_T):
            y_ref[rb * 8:(rb + 1) * 8, t, :] = y[rb * S_BLK + t * 8:rb * S_BLK + t * 8 + 8, :]


def _sample_back_call(x3, p3, num, dec, den, qc, og, zg, xc, yb, weights, vecs):
    (_, _, _, _, _, _, _, _, wout, wple, wpg) = weights
    (_, _, _, _, _, lnhead, skip, _, _, lnple, lnfin) = vecs
    return pl.pallas_call(
        _sample_back_body, out_shape=jax.ShapeDtypeStruct(x3.shape, F32),
        scratch_shapes=[pltpu.VMEM((ROWS, D_MODEL), F32), pltpu.VMEM((ROWS, D_PLE), F32)],
        compiler_params=pltpu.CompilerParams(vmem_limit_bytes=VMEM_LIMIT),
        name="sample_back",
    )(x3, p3, num, dec, den, qc, og, zg, xc, yb, wout, wple, wpg, lnhead, skip, lnple, lnfin)


def _sample_path(x_sample, p_sample, c_state, n_state, m_state, conv_state, sre_state, sim_state,
                 weights, vecs, are8, aim8):
    nb, n_t, _ = x_sample.shape
    assert nb == S_RB * 8 and n_t == S_T
    m_rows = jnp.broadcast_to(jnp.transpose(m_state).reshape(N_HEADS, S_RB, 1, 8), (N_HEADS, S_RB, S_T, 8))
    mrow_in = jnp.pad(m_rows.reshape(N_HEADS, ROWS), ((0, 8 - N_HEADS), (0, 0)))
    s5_t = lambda s: jnp.transpose(s, (1, 2, 0)).reshape(N_STATE, nb)
    (q_bf, k_bf, wlv_bf, colt, num, dec, den, og, zg, xc, yb, n_new, mrow, conv_new, sre, sim) = _sample_front_call(
        x_sample, weights, vecs, are8, aim8, n_state, mrow_in, jnp.transpose(conv_state, (1, 0, 2)),
        s5_t(sre_state), s5_t(sim_state))
    qc, c_new = _sample_state_call(c_state, q_bf, k_bf, wlv_bf, colt)
    y_s = _sample_back_call(x_sample, p_sample, num, dec, den, qc, og, zg, xc, yb, weights, vecs)
    m_new = jnp.transpose(mrow[0:N_HEADS].reshape(N_HEADS, S_RB, S_T, 8)[:, :, 0, :].reshape(N_HEADS, nb))
    s5_back = lambda s: jnp.transpose(s.reshape(N_GROUPS, S5_STATE, nb), (2, 0, 1))[None]
    return (y_s, c_new, n_new[None], m_new[None], jnp.transpose(conv_new, (1, 0, 2))[None],
            s5_back(sre), s5_back(sim))


IN_SPLITS = (0, MIX, 2 * MIX, 3 * MIX + 2 * N_HEADS, 4 * MIX + 2 * N_HEADS)
GATE_ROW = 3 * MIX


def _weight_prep_body(wint_ref, wq_ref, wk_ref, wv_ref, wglu_ref, wout_ref, wple_ref, wpg_ref,
                      wbig_ref, wgate_ref, wqk_ref, wvo_ref, wgluo_ref, wouto_ref, wpleo_ref, wpgo_ref):
    for s, r0 in enumerate(IN_SPLITS):
        wbig_ref[:, s * MIX:(s + 1) * MIX] = wint_ref[r0:r0 + MIX, :].T.astype(BF16)
    g8 = wint_ref[GATE_ROW:GATE_ROW + 8, :]
    row = lax.broadcasted_iota(jnp.int32, g8.shape, 0)
    zero = jnp.zeros_like(g8)
    top = jnp.where(row < N_HEADS, g8, zero)
    bot = jnp.where(row < N_HEADS, pltpu.roll(g8, N_HEADS, axis=0), zero)
    wgate_ref[...] = jnp.concatenate([top, bot], axis=0).astype(BF16)
    for h in range(N_HEADS):
        wqk_ref[h, :, 0:HEAD_DIM] = wq_ref[0, h].astype(BF16)
        wqk_ref[h, :, HEAD_DIM:2 * HEAD_DIM] = wk_ref[0, h].astype(BF16)
    wvo_ref[...] = wv_ref[0].astype(BF16)
    wgluo_ref[...] = wglu_ref[0].astype(BF16)
    wouto_ref[...] = wout_ref[0].astype(BF16)
    wpleo_ref[...] = wple_ref[0].astype(BF16)
    wpgo_ref[...] = wpg_ref[0].astype(BF16)


def _weight_prep_call(w_in_t, w_q, w_k, w_v, w_glu, w_out, w_ple, w_pg):
    bf = lambda *s: jax.ShapeDtypeStruct(s, BF16)
    return pl.pallas_call(
        _weight_prep_body,
        out_shape=(bf(D_MODEL, 5 * MIX), bf(16, D_MODEL), bf(N_HEADS, HEAD_DIM, 2 * HEAD_DIM),
                   bf(N_HEADS, HEAD_DIM, HEAD_DIM), bf(MIX, MIX), bf(2 * MIX, D_MODEL), bf(D_PLE, D_MODEL),
                   bf(D_MODEL, D_MODEL)),
        compiler_params=pltpu.CompilerParams(vmem_limit_bytes=VMEM_LIMIT),
        name="weight_prep",
    )(w_in_t, w_q, w_k, w_v, w_glu, w_out, w_ple, w_pg)


def _layout_params(ln_mix, w_in, b_igate, b_fgate, conv_w, conv_b, w_q, w_k, w_v, ln_head, skip_a,
                   s5_lam_re, s5_lam_im, s5_log_dt, s5_B_re, s5_B_im, s5_C_re, s5_C_im, s5_D, w_glu, b_glu,
                   w_out, w_ple, ln_ple, w_ple_gate, ln_final):
    wbig, wgate, wqk, wv, wglu_bf, wout_bf, wple_bf, wpg_bf = _weight_prep_call(
        jnp.transpose(w_in[0]), w_q, w_k, w_v, w_glu, w_out, w_ple, w_ple_gate)
    b_rep = lambda m: jnp.tile(m[0].reshape(2, 16 * S5_STATE, S5_GROUP), (1, 1, 16))
    c_rep = lambda m: jnp.tile(m[0].reshape(2, 16 * S5_GROUP, S5_STATE), (1, 1, 16))
    lam = jnp.concatenate([v.reshape(1, N_STATE) for v in (s5_lam_re, s5_lam_im, s5_log_dt)], axis=0)
    are8, aim8, wb, wcre, wcim = _prep_call(lam, b_rep(s5_B_re), b_rep(s5_B_im), c_rep(s5_C_re), c_rep(s5_C_im))
    weights = [wbig, wgate, wqk, wv, wb, wcre, wcim, wglu_bf, wout_bf, wple_bf, wpg_bf]
    pad8 = lambda v: jnp.pad(v[0][:, None], ((0, 8 - N_HEADS), (0, 0)))
    vecs = [ln_mix[0][None, :], conv_w[0], conv_b[0][None, :], pad8(b_igate), pad8(b_fgate),
            ln_head[0][None, :], skip_a[0][None, :], s5_D[0][None, :], b_glu[0][None, :],
            ln_ple[0][None, :], ln_final[None, :]]
    return weights, vecs, are8, aim8


def kernel(x_prompt, x_sample, p_prompt, p_sample, state_mlstm_C, state_mlstm_n, state_mlstm_m, state_conv, state_s5_re, state_s5_im, ln_mix, w_in, b_igate, b_fgate, conv_w, conv_b, w_q, w_k, w_v, ln_head, skip_a, s5_lam_re, s5_lam_im, s5_log_dt, s5_B_re, s5_B_im, s5_C_re, s5_C_im, s5_D, w_glu, b_glu, w_out, w_ple, ln_ple, w_ple_gate, ln_final):
    weights, vecs, are8, aim8 = _layout_params(
        ln_mix, w_in, b_igate, b_fgate, conv_w, conv_b, w_q, w_k, w_v, ln_head, skip_a, s5_lam_re, s5_lam_im,
        s5_log_dt, s5_B_re, s5_B_im, s5_C_re, s5_C_im, s5_D, w_glu, b_glu, w_out, w_ple, ln_ple, w_ple_gate,
        ln_final)

    nbp = x_prompt.shape[0]
    y_p, c_p, n_p, mrow_p, conv_t, sre_p, sim_p = _prompt_call(x_prompt, p_prompt[0], weights, vecs, are8, aim8)
    conv_p = jnp.transpose(conv_t, (1, 0, 2))[None]
    pn = n_p.reshape(1, nbp, N_HEADS, HEAD_DIM)
    pm = jnp.transpose(mrow_p[0:N_HEADS, ::P_T])[None]
    pre = sre_p.reshape(1, nbp, N_GROUPS, S5_STATE)
    pim = sim_p.reshape(1, nbp, N_GROUPS, S5_STATE)

    y_s, c_s, n_s, m_s, conv_s, sre_s, sim_s = _sample_path(
        x_sample, p_sample[0], state_mlstm_C, state_mlstm_n[0], state_mlstm_m[0], state_conv[0],
        state_s5_re[0], state_s5_im[0], weights, vecs, are8, aim8)
    return (y_p, y_s, c_p, pn, pm, conv_p, pre, pim, c_s, n_s, m_s, conv_s, sre_s, sim_s)
```

```python
import functools
import math

import jax
import jax.numpy as jnp
from jax import lax
from jax.experimental import pallas as pl
from jax.experimental.pallas import tpu as pltpu

F32 = jnp.float32
BF16 = jnp.bfloat16

D_MODEL = 1024
D_PLE = 256
MIX = 512
N_HEADS = 4
HEAD_DIM = 128
N_GROUPS = 32
S5_GROUP = 16
S5_STATE = 64
N_STATE = N_GROUPS * S5_STATE
CONV_W = 4
EPS = 1e-6
K_SCALE = HEAD_DIM ** -0.5
ROWS = 512
GROUP_ROWS = 256
VMEM_LIMIT = 60 * 1024 * 1024


def _mm(a, w):
    return jnp.dot(a, w, preferred_element_type=F32)


def _rms(x, g):
    ms = jnp.mean(x * x, axis=-1, keepdims=True)
    return x * lax.rsqrt(ms + EPS) * g


def _sigmoid(x):
    return 0.5 * jnp.tanh(0.5 * x) + 0.5


def _silu(x):
    h = 0.5 * x
    return h + h * jnp.tanh(h)


def _gelu_tanh(x):
    c = math.sqrt(2.0 / math.pi)
    return 0.5 * x * (1.0 + jnp.tanh(c * (x + 0.044715 * (x * x * x))))


def _log_sigmoid(x):
    return jnp.minimum(x, 0.0) - jnp.log1p(jnp.exp(-jnp.abs(x)))


SCAN_RADIX = 8


def _seg_scan(x, tidx, n_t, t_stride, op, fill):
    span = 1
    while span < n_t:
        acc = x
        for j in range(1, SCAN_RADIX):
            if j * span >= n_t:
                break
            prev = pltpu.roll(x, j * span * t_stride, axis=1)
            acc = op(acc, jnp.where(tidx >= j * span, prev, fill))
        x = acc
        span *= SCAN_RADIX
    return x


def _bcast_last(x, tidx, n_t, t_stride):
    y = jnp.where(tidx == n_t - 1, x, 0.0)
    span = 1
    while span < n_t:
        acc = y
        for j in range(1, SCAN_RADIX):
            if j * span >= n_t:
                break
            nxt = pltpu.roll(y, ROWS - j * span * t_stride, axis=1)
            acc = acc + jnp.where(tidx + j * span <= n_t - 1, nxt, 0.0)
        y = acc
        span *= SCAN_RADIX
    return y


def _gate_rows(a, wg_ref, bi8, bf8, mprev, n_t, t_stride):
    gt = lax.dot_general(wg_ref[...], a, (((1,), (1,)), ((), ())), preferred_element_type=F32)
    lane = lax.broadcasted_iota(jnp.int32, (8, ROWS), 1)
    tidx = (lane // t_stride) % n_t
    i8 = gt[0:8] + bi8
    logf = _log_sigmoid(gt[8:16] + bf8)
    b = _seg_scan(logf, tidx, n_t, t_stride, jnp.add, 0.0)
    c = i8 - b
    cm = _seg_scan(c, tidx, n_t, t_stride, jnp.maximum, -jnp.inf)
    big_m = jnp.maximum(mprev, cm)
    mt = b + big_m
    m_last = _bcast_last(big_m, tidx, n_t, t_stride)
    mnew = _bcast_last(mt, tidx, n_t, t_stride)
    stack = jnp.concatenate(
        [-big_m, jnp.exp(mprev - big_m), jnp.exp(-mt), jnp.exp(c - m_last), jnp.exp(mprev - m_last),
         jnp.zeros((128 - 40, ROWS), F32)], axis=0)
    return c, stack.T, mnew


COL_NEGM, COL_DECAY, COL_EM, COL_WL, COL_DL = 0, 8, 16, 24, 32


def _intra(qg, kg, vg, col_a, row_b, mask):
    s = lax.dot_general(qg, kg, (((1,), (1,)), ((), ())), preferred_element_type=F32)
    p = s * jnp.exp(jnp.where(mask, col_a + row_b, -jnp.inf))
    rs = jnp.sum(p, axis=-1, keepdims=True)
    return _mm(p.astype(BF16), vg), rs


def _head_norm(x):
    mu = jnp.mean(x, axis=-1, keepdims=True)
    xc = x - mu
    var = jnp.mean(xc * xc, axis=-1, keepdims=True)
    return xc * lax.rsqrt(var + EPS)


S5_HALF = N_STATE // 2


S5_CHUNK = 512
S5_N_CHUNKS = N_STATE // S5_CHUNK


def _s5_scan(views, are_ref, aim_ref, sre_ref, sim_ref, row_base, st_rows, n_t, chunks=range(S5_N_CHUNKS)):
    chunk = S5_CHUNK
    assert n_t % 2 == 0 and row_base % 16 == 0
    for gch in chunks:
        hf, ch = divmod(gch, S5_HALF // chunk)
        bure_s, buim_s, xsre_s, xsim_s = views[hf]
        ls = slice(ch * chunk, (ch + 1) * chunk)
        gl = slice(hf * S5_HALF + ch * chunk, hf * S5_HALF + (ch + 1) * chunk)
        ar = are_ref[:, gl]
        ai = aim_ref[:, gl]
        xr = sre_ref[st_rows, gl]
        xi = sim_ref[st_rows, gl]
        for t in range(0, n_t, 2):
            r0 = row_base + t * 8
            xr1, xi1 = (ar * xr - ai * xi + bure_s[r0:r0 + 8, ls], ar * xi + ai * xr + buim_s[r0:r0 + 8, ls])
            xr, xi = (ar * xr1 - ai * xi1 + bure_s[r0 + 8:r0 + 16, ls],
                      ar * xi1 + ai * xr1 + buim_s[r0 + 8:r0 + 16, ls])
            xsre_s[r0:r0 + 16, ls] = jnp.concatenate([xr1, xr], axis=0).astype(BF16)
            xsim_s[r0:r0 + 16, ls] = jnp.concatenate([xi1, xi], axis=0).astype(BF16)
        sre_ref[st_rows, gl] = xr
        sim_ref[st_rows, gl] = xi


def _s5_views(bure_s, buim_s, xsre_s, xsim_s):
    return [[r.at[hf] for r in (bure_s, buim_s, xsre_s, xsim_s)] for hf in range(2)]


def _s5_in(views, u_tm, wb_ref):
    for hf in range(2):
        res = _mm(u_tm[:, hf * 256:(hf + 1) * 256].astype(BF16), wb_ref[hf])
        views[hf][0][...] = res[:, :S5_HALF]
        views[hf][1][...] = res[:, S5_HALF:]


def _s5_out(views, wcre_ref, wcim_ref):
    return [_mm(views[hf][2][...], wcre_ref[hf]) - _mm(views[hf][3][...], wcim_ref[hf]) for hf in range(2)]


def _s5_post(y_s5, u, a, wbig_ref, wglu_ref, s5d_ref, bglu_ref):
    yb = _gelu_tanh(y_s5 + s5d_ref[...] * u)
    yb = yb * _sigmoid(_mm(yb.astype(BF16), wglu_ref[...]) + bglu_ref[...])
    return yb * _silu(_mm(a, wbig_ref[:, 2048:2560]))


OUT_BLOCK = 256


def _merge_out(x, p, h_a, y_b, wout_ref, wple_ref, wpg_ref, lnple_ref, lnfin_ref):
    h1 = x + _mm(h_a.astype(BF16), wout_ref[0:MIX, :]) + _mm(y_b.astype(BF16), wout_ref[MIX:2 * MIX, :])
    e_half = 0.5 * _rms(_mm(p.astype(BF16), wple_ref[...]), lnple_ref[...])
    h1_bf = h1.astype(BF16)
    base = h1 + e_half
    blocks = []
    ssq = jnp.zeros((h1.shape[0], 1), F32)
    for nb in range(D_MODEL // OUT_BLOCK):
        cs = slice(nb * OUT_BLOCK, (nb + 1) * OUT_BLOCK)
        h2 = base[:, cs] + e_half[:, cs] * jnp.tanh(0.5 * _mm(h1_bf, wpg_ref[:, cs]))
        ssq = ssq + jnp.sum(h2 * h2, axis=-1, keepdims=True)
        blocks.append(h2)
    scale = lax.rsqrt(ssq * (1.0 / D_MODEL) + EPS)
    return jnp.concatenate(blocks, axis=-1) * scale * lnfin_ref[...]


def _prep_body(lam_ref, btr_ref, bti_ref, ctr_ref, cti_ref, are_ref, aim_ref, wb_ref, wcre_ref, wcim_ref):
    lr = jnp.minimum(lam_ref[0:1, :], -1e-4)
    li = lam_ref[1:2, :]
    dt = jnp.exp(lam_ref[2:3, :])
    mag = jnp.exp(lr * dt)
    a_re = mag * jnp.cos(li * dt)
    a_im = mag * jnp.sin(li * dt)
    den = lr * lr + li * li
    xr = a_re - 1.0
    g_re = (xr * lr + a_im * li) / den
    g_im = (a_im * lr - xr * li) / den
    are_ref[...] = jnp.broadcast_to(a_re, (8, N_STATE))
    aim_ref[...] = jnp.broadcast_to(a_im, (8, N_STATE))
    half = N_STATE // 2
    own_block = ((lax.broadcasted_iota(jnp.int32, (256, half), 0) // S5_GROUP)
                 == (lax.broadcasted_iota(jnp.int32, (256, half), 1) // S5_STATE))
    blocks = lambda ref, hf: jnp.where(own_block, jnp.concatenate([ref[hf]] * 16, axis=1), 0.0)
    for hf in range(2):
        gr = g_re[:, hf * half:(hf + 1) * half]
        gi = g_im[:, hf * half:(hf + 1) * half]
        br = blocks(btr_ref, hf)
        bi = blocks(bti_ref, hf)
        wb_ref[hf, :, 0:half] = (gr * br - gi * bi).astype(BF16)
        wb_ref[hf, :, half:2 * half] = (gr * bi + gi * br).astype(BF16)
        wcre_ref[hf] = blocks(ctr_ref, hf).T.astype(BF16)
        wcim_ref[hf] = blocks(cti_ref, hf).T.astype(BF16)


def _prep_call(lam, bt_re, bt_im, ct_re, ct_im):
    half = N_STATE // 2
    return pl.pallas_call(
        _prep_body,
        out_shape=(jax.ShapeDtypeStruct((8, N_STATE), F32), jax.ShapeDtypeStruct((8, N_STATE), F32),
                   jax.ShapeDtypeStruct((2, 256, N_STATE), BF16),
                   jax.ShapeDtypeStruct((2, half, 256), BF16), jax.ShapeDtypeStruct((2, half, 256), BF16)),
        compiler_params=pltpu.CompilerParams(vmem_limit_bytes=VMEM_LIMIT),
        name="s5_prep",
    )(lam, bt_re, bt_im, ct_re, ct_im)


P_T = 64
P_NB = 8


P_SUB = 2


def _prompt_body(*refs):
    n_in = 26
    c_ref, n_ref, mrow_ref = refs[n_in + 1:n_in + 4]
    sre_ref, sim_ref, ext_s = refs[n_in + 5:n_in + 8]
    assert len(refs) == n_in + 7 + 9

    @pl.when(pl.program_id(0) == 0)
    def _init():
        c_ref[...] = jnp.zeros_like(c_ref)
        n_ref[...] = jnp.zeros_like(n_ref)
        mrow_ref[...] = jnp.zeros_like(mrow_ref)
        sre_ref[...] = jnp.zeros_like(sre_ref)
        sim_ref[...] = jnp.zeros_like(sim_ref)
        ext_s[:, 0:8, :] = jnp.zeros((P_NB, 8, MIX), F32)

    def one_tile(sub, carry):
        _prompt_tile(pl.ds(pl.multiple_of(sub * P_T, P_T), P_T), *refs)
        return carry

    lax.fori_loop(0, P_SUB, one_tile, 0)


def _prompt_tile(ts, x_ref, p_ref, wbig_ref, wgate_ref, wqk_ref, wv_ref, wb_ref, wcre_ref, wcim_ref,
                 wglu_ref, wout_ref, wple_ref, wpg_ref, lnmix_ref, convw_ref, convb_ref, bi8_ref, bf8_ref,
                 lnhead_ref, skip_ref, s5d_ref, bglu_ref, lnple_ref, lnfin_ref, are_ref, aim_ref,
                 y_ref, c_ref, n_ref, mrow_ref, conv_ref, sre_ref, sim_ref,
                 ext_s, ha_s, utm_s, ytm_s, ybm_s, bure_s, buim_s, xsre_s, xsim_s):
    x = x_ref[:, ts, :].reshape(ROWS, D_MODEL)
    a = _rms(x, lnmix_ref[...]).astype(BF16)

    c8, colt, mnew = _gate_rows(a, wgate_ref, bi8_ref[...], bf8_ref[...], mrow_ref[...], P_T, 1)
    mrow_ref[...] = mnew

    pair_w = 2 * HEAD_DIM
    xc_parts, xc_bf_parts, xm_bf_parts = [], [], []
    for pr in range(N_HEADS // 2):
        ps = slice(pr * pair_w, (pr + 1) * pair_w)
        xm = _mm(a, wbig_ref[:, ps])
        ext_s[:, 8:8 + P_T, ps] = xm.reshape(P_NB, P_T, pair_w)
        yc = convb_ref[:, ps]
        for j in range(CONV_W):
            yc = yc + ext_s[:, 5 + j:5 + j + P_T, ps].reshape(ROWS, pair_w) * convw_ref[j:j + 1, ps]
        for j in range(CONV_W - 1):
            conv_ref[j, :, ps] = ext_s[:, 5 + P_T + j, ps]
        ext_s[:, 5:8, ps] = ext_s[:, 5 + P_T:8 + P_T, ps]
        xc_pr = _silu(yc)
        xc_parts.append(xc_pr)
        xc_bf_parts.append(xc_pr.astype(BF16))
        xm_bf_parts.append(xm.astype(BF16))
    xc = jnp.concatenate(xc_parts, axis=-1)

    r = lax.broadcasted_iota(jnp.int32, (GROUP_ROWS, GROUP_ROWS), 0)
    cc = lax.broadcasted_iota(jnp.int32, (GROUP_ROWS, GROUP_ROWS), 1)
    mask = ((r // P_T) == (cc // P_T)) & (cc <= r)

    for h in range(N_HEADS):
        hs = slice(h * HEAD_DIM, (h + 1) * HEAD_DIM)
        hp = slice((h % 2) * HEAD_DIM, (h % 2 + 1) * HEAD_DIM)
        qk = _mm(xc_bf_parts[h // 2][:, hp], wqk_ref[h])
        q_f = qk[:, :HEAD_DIM]
        k_f = qk[:, HEAD_DIM:] * K_SCALE
        v_f = _mm(xm_bf_parts[h // 2][:, hp], wv_ref[h])
        q_bf = q_f.astype(BF16)
        k_bf = k_f.astype(BF16)
        v_bf = v_f.astype(BF16)
        for g in range(ROWS // GROUP_ROWS):
            gs = slice(g * GROUP_ROWS, (g + 1) * GROUP_ROWS)
            decay = colt[gs, COL_DECAY + h:COL_DECAY + h + 1]
            wl = colt[gs, COL_WL + h:COL_WL + h + 1]
            dl = colt[gs, COL_DL + h:COL_DL + h + 1]
            num, rs = _intra(q_bf[gs], k_bf[gs], v_bf[gs], colt[gs, COL_NEGM + h:COL_NEGM + h + 1],
                             c8[h:h + 1, gs], mask)
            wlv = (wl * v_f[gs]).astype(BF16)
            wlk = wl * k_f[gs]
            qcs, qns = [], []
            for bb in range(GROUP_ROWS // P_T):
                b = g * (GROUP_ROWS // P_T) + bb
                rb = slice(g * GROUP_ROWS + bb * P_T, g * GROUP_ROWS + (bb + 1) * P_T)
                rl = slice(bb * P_T, (bb + 1) * P_T)
                c_old = c_ref[0, b, h]
                n_old = n_ref[b:b + 1, hs]
                qcs.append(_mm(q_bf[rb], c_old.astype(BF16)))
                qns.append(jnp.sum(q_f[rb] * n_old, axis=-1, keepdims=True))
                upd = lax.dot_general(k_bf[rb], wlv[rl], (((0,), (0,)), ((), ())), preferred_element_type=F32)
                dlb = dl[rl]
                c_ref[0, b, h] = jnp.concatenate([dlb, dlb], axis=0) * c_old + upd
                n_ref[b:b + 1, hs] = dlb[0:1] * n_old + jnp.sum(wlk[rl], axis=0, keepdims=True)
            num = num + decay * jnp.concatenate(qcs, axis=0)
            qn = rs + decay * jnp.concatenate(qns, axis=0)
            den = jnp.maximum(jnp.abs(qn), colt[gs, COL_EM + h:COL_EM + h + 1])
            ha_s[gs, hs] = num / den

    ha = ha_s[...] * _sigmoid(_mm(a, wbig_ref[:, 1024:1536]))
    ha = jnp.concatenate([_head_norm(ha[:, h * HEAD_DIM:(h + 1) * HEAD_DIM]) for h in range(N_HEADS)], axis=-1)
    ha = (ha * lnhead_ref[...] + skip_ref[...] * xc) * _silu(_mm(a, wbig_ref[:, 512:1024]))

    u = _mm(a, wbig_ref[:, 1536:2048])
    for b in range(P_NB):
        for j in range(P_T // 8):
            for lc in range(MIX // 128):
                utm_s[lc, pl.ds(j * 64 + b, 8, stride=8), :] = (
                    u[b * P_T + j * 8:b * P_T + j * 8 + 8, lc * 128:(lc + 1) * 128])
    views = _s5_views(bure_s, buim_s, xsre_s, xsim_s)
    _s5_in(views, jnp.concatenate([utm_s[lc] for lc in range(MIX // 128)], axis=-1), wb_ref)
    _s5_scan(views, are_ref, aim_ref, sre_ref, sim_ref, 0, slice(0, 8), P_T)
    y_halves = _s5_out(views, wcre_ref, wcim_ref)
    for hf, y_half in enumerate(y_halves):
        ytm_s[2 * hf] = y_half[:, 0:128]
        ytm_s[2 * hf + 1] = y_half[:, 128:256]
    for b in range(P_NB):
        for j in range(P_T // 8):
            for lc in range(MIX // 128):
                ybm_s[b * P_T + j * 8:b * P_T + j * 8 + 8, lc * 128:(lc + 1) * 128] = (
                    ytm_s[lc, pl.ds(j * 64 + b, 8, stride=8), :])
    yb = _s5_post(ybm_s[...], u, a, wbig_ref, wglu_ref, s5d_ref, bglu_ref)

    p = p_ref[:, ts, :].reshape(ROWS, D_PLE)
    y = _merge_out(x, p, ha, yb, wout_ref, wple_ref, wpg_ref, lnple_ref, lnfin_ref)
    y_ref[:, ts, :] = y.reshape(P_NB, P_T, D_MODEL)


def _const_spec(shape):
    nd = len(shape)
    return pl.BlockSpec(shape, lambda i, nd=nd: (0,) * nd, pipeline_mode=pl.Buffered(1))


def _prompt_call(x, p, weights, vecs, are8, aim8):
    nb, seq, _ = x.shape
    blk_t = P_T * P_SUB
    assert nb == P_NB and seq % blk_t == 0
    n_steps = seq // blk_t
    w_specs = [_const_spec(w.shape) for w in weights]
    v_specs = [_const_spec(v.shape) for v in vecs]
    in_specs = ([pl.BlockSpec((P_NB, blk_t, D_MODEL), lambda i: (0, i, 0)),
                 pl.BlockSpec((P_NB, blk_t, D_PLE), lambda i: (0, i, 0))]
                + w_specs + v_specs + [_const_spec((8, N_STATE)), _const_spec((8, N_STATE))])
    out_shape = (
        jax.ShapeDtypeStruct((nb, seq, D_MODEL), F32),
        jax.ShapeDtypeStruct((1, nb, N_HEADS, HEAD_DIM, HEAD_DIM), F32),
        jax.ShapeDtypeStruct((nb, MIX), F32),
        jax.ShapeDtypeStruct((8, ROWS), F32),
        jax.ShapeDtypeStruct((CONV_W - 1, nb, MIX), F32),
        jax.ShapeDtypeStruct((8, N_STATE), F32),
        jax.ShapeDtypeStruct((8, N_STATE), F32),
    )
    out_specs = (
        pl.BlockSpec((P_NB, blk_t, D_MODEL), lambda i: (0, i, 0)),
        pl.BlockSpec((1, nb, N_HEADS, HEAD_DIM, HEAD_DIM), lambda i: (0, 0, 0, 0, 0)),
        pl.BlockSpec((nb, MIX), lambda i: (0, 0)),
        pl.BlockSpec((8, ROWS), lambda i: (0, 0)),
        pl.BlockSpec((CONV_W - 1, nb, MIX), lambda i: (0, 0, 0)),
        pl.BlockSpec((8, N_STATE), lambda i: (0, 0)),
        pl.BlockSpec((8, N_STATE), lambda i: (0, 0)),
    )
    scratch = [
        pltpu.VMEM((P_NB, P_T + 8, MIX), F32),
        pltpu.VMEM((ROWS, MIX), F32),
        pltpu.VMEM((MIX // 128, ROWS, 128), F32),
        pltpu.VMEM((MIX // 128, ROWS, 128), F32),
        pltpu.VMEM((ROWS, MIX), F32),
        pltpu.VMEM((2, ROWS, S5_HALF), F32),
        pltpu.VMEM((2, ROWS, S5_HALF), F32),
        pltpu.VMEM((2, ROWS, S5_HALF), BF16),
        pltpu.VMEM((2, ROWS, S5_HALF), BF16),
    ]
    return pl.pallas_call(
        _prompt_body,
        grid=(n_steps,),
        in_specs=in_specs,
        out_specs=out_specs,
        out_shape=out_shape,
        scratch_shapes=scratch,
        compiler_params=pltpu.CompilerParams(dimension_semantics=("arbitrary",), vmem_limit_bytes=VMEM_LIMIT),
        name="prompt_layer",
    )(x, p, *weights, *vecs, are8, aim8)


S_T = 4
S_RB = ROWS // (8 * S_T)
S_BLK = 8 * S_T


def _sample_front_body(x_ref, wbig_ref, wgate_ref, wqk_ref, wv_ref, wb_ref, wcre_ref, wcim_ref, wglu_ref,
                       lnmix_ref, convw_ref, convb_ref, bi8_ref, bf8_ref, s5d_ref, bglu_ref, are_ref, aim_ref,
                       n_in_ref, mrow_in_ref, conv_in_ref, sre_in_ref, sim_in_ref,
                       q_ref, k_ref, wlv_ref, colt_ref, num_ref, dec_ref, den_ref, og_ref, zg_ref, xc_ref, yb_ref,
                       n_ref, mrow_ref, conv_ref, sre_out_ref, sim_out_ref,
                       ext_s, bure_s, buim_s, xsre_s, xsim_s, xtm_s, sre_ref, sim_ref):
    x = _rows_to_blocks(x_ref, xtm_s)
    a = _rms(x, lnmix_ref[...]).astype(BF16)

    xm = _mm(a, wbig_ref[:, 0:512])
    ext_rows = (CONV_W - 1 + S_T) * 8
    ycs = []
    for rb in range(S_RB):
        seqs = slice(rb * 8, (rb + 1) * 8)
        for j in range(CONV_W - 1):
            ext_s[rb * ext_rows + 8 * j:rb * ext_rows + 8 * j + 8, :] = conv_in_ref[j, seqs, :]
        ext_s[rb * ext_rows + 24:(rb + 1) * ext_rows, :] = xm[rb * S_BLK:(rb + 1) * S_BLK, :]
        yc = convb_ref[...]
        for j in range(CONV_W):
            yc = yc + ext_s[rb * ext_rows + 8 * j:rb * ext_rows + 8 * j + S_BLK, :] * convw_ref[j:j + 1, :]
        ycs.append(yc)
        for j in range(CONV_W - 1):
            conv_ref[j, seqs, :] = ext_s[rb * ext_rows + S_BLK + 8 * j:rb * ext_rows + S_BLK + 8 * j + 8, :]
    xc = _silu(jnp.concatenate(ycs, axis=0))
    xc_ref[...] = xc
    xc_bf = xc.astype(BF16)
    xm_bf = xm.astype(BF16)

    mprev = mrow_in_ref[...]
    c8, colt, mnew = _gate_rows(a, wgate_ref, bi8_ref[...], bf8_ref[...], mprev, S_T, 8)
    mrow_ref[...] = mnew
    colt_ref[...] = colt

    r = lax.broadcasted_iota(jnp.int32, (GROUP_ROWS, GROUP_ROWS), 0)
    cc = lax.broadcasted_iota(jnp.int32, (GROUP_ROWS, GROUP_ROWS), 1)
    same_seq = ((r // S_BLK) == (cc // S_BLK)) & ((r % 8) == (cc % 8))
    mask = same_seq & (((cc // 8) % S_T) <= ((r // 8) % S_T))

    nb = S_RB * 8
    for h in range(N_HEADS):
        hs = slice(h * HEAD_DIM, (h + 1) * HEAD_DIM)
        qk = _mm(xc_bf[:, hs], wqk_ref[h])
        q_f = qk[:, :HEAD_DIM]
        k_f = qk[:, HEAD_DIM:] * K_SCALE
        v_f = _mm(xm_bf[:, hs], wv_ref[h])
        q_bf = q_f.astype(BF16)
        k_bf = k_f.astype(BF16)
        v_bf = v_f.astype(BF16)
        decay = colt[:, COL_DECAY + h:COL_DECAY + h + 1]
        wl = colt[:, COL_WL + h:COL_WL + h + 1]
        dl = colt[:, COL_DL + h:COL_DL + h + 1]
        nums, rss = [], []
        for g in range(ROWS // GROUP_ROWS):
            gs = slice(g * GROUP_ROWS, (g + 1) * GROUP_ROWS)
            num, rs = _intra(q_bf[gs], k_bf[gs], v_bf[gs], colt[gs, COL_NEGM + h:COL_NEGM + h + 1],
                             c8[h:h + 1, gs], mask)
            nums.append(num)
            rss.append(rs)
        q_ref[:, hs] = q_bf
        k_ref[:, hs] = k_bf
        wlv_ref[:, hs] = (wl * v_f).astype(BF16)
        num_ref[:, hs] = jnp.concatenate(nums, axis=0)
        n_old = n_in_ref[:, h, :]
        n_rows = jnp.broadcast_to(n_old.reshape(S_RB, 1, 8, HEAD_DIM), (S_RB, S_T, 8, HEAD_DIM)).reshape(ROWS, HEAD_DIM)
        qn = jnp.concatenate(rss, axis=0) + decay * jnp.sum(q_f * n_rows, axis=-1, keepdims=True)
        den = jnp.maximum(jnp.abs(qn), colt[:, COL_EM + h:COL_EM + h + 1])
        dec_ref[:, hs] = jnp.broadcast_to(decay, (ROWS, HEAD_DIM))
        den_ref[:, hs] = jnp.broadcast_to(den, (ROWS, HEAD_DIM))
        wlk_sum = jnp.sum((wl * k_f).reshape(S_RB, S_T, 8, HEAD_DIM), axis=1).reshape(nb, HEAD_DIM)
        dl_seq = dl.reshape(S_RB, S_T, 8, 1)[:, 0].reshape(nb, 1)
        n_ref[:, h, :] = dl_seq * n_old + wlk_sum

    og_ref[...] = _sigmoid(_mm(a, wbig_ref[:, 1024:1536]))
    zg_ref[...] = _silu(_mm(a, wbig_ref[:, 512:1024]))

    u = _mm(a, wbig_ref[:, 1536:2048])
    sre_ref[...] = sre_in_ref[...].T
    sim_ref[...] = sim_in_ref[...].T
    views = _s5_views(bure_s, buim_s, xsre_s, xsim_s)
    _s5_in(views, u, wb_ref)
    for rb in range(S_RB):
        _s5_scan(views, are_ref, aim_ref, sre_ref, sim_ref, rb * S_BLK, slice(rb * 8, (rb + 1) * 8), S_T)
    y_s5 = jnp.concatenate(_s5_out(views, wcre_ref, wcim_ref), axis=-1)
    sre_out_ref[...] = sre_ref[...].T
    sim_out_ref[...] = sim_ref[...].T
    yb_ref[...] = _s5_post(y_s5, u, a, wbig_ref, wglu_ref, s5d_ref, bglu_ref)


def _rows_to_blocks(src_ref, dst_s):
    for rb in range(S_RB):
        for t in range(S_T):
            dst_s[rb * S_BLK + t * 8:rb * S_BLK + t * 8 + 8, :] = src_ref[rb * 8:(rb + 1) * 8, t, :]
    return dst_s[...]


def _sample_front_call(x3, weights, vecs, are8, aim8, n3, mrow_in, conv_t, sre_t, sim_t):
    (wbig, wgate, wqk, wv, wb, wcre, wcim, wglu, _, _, _) = weights
    (lnmix, convw, convb, bi8, bf8, _, _, s5d, bglu, _, _) = vecs
    nb = S_RB * 8
    f32 = lambda *s: jax.ShapeDtypeStruct(s, F32)
    bf = lambda *s: jax.ShapeDtypeStruct(s, BF16)
    out_shape = (bf(ROWS, MIX), bf(ROWS, MIX), bf(ROWS, MIX), f32(ROWS, 128), f32(ROWS, MIX), f32(ROWS, MIX),
                 f32(ROWS, MIX), f32(ROWS, MIX), f32(ROWS, MIX), f32(ROWS, MIX), f32(ROWS, MIX),
                 f32(nb, N_HEADS, HEAD_DIM), f32(8, ROWS), f32(CONV_W - 1, nb, MIX), f32(N_STATE, nb),
                 f32(N_STATE, nb))
    scratch = [pltpu.VMEM((S_RB * (CONV_W - 1 + S_T) * 8, MIX), F32),
               pltpu.VMEM((2, ROWS, S5_HALF), F32), pltpu.VMEM((2, ROWS, S5_HALF), F32),
               pltpu.VMEM((2, ROWS, S5_HALF), BF16), pltpu.VMEM((2, ROWS, S5_HALF), BF16),
               pltpu.VMEM((ROWS, D_MODEL), F32), pltpu.VMEM((nb, N_STATE), F32), pltpu.VMEM((nb, N_STATE), F32)]
    return pl.pallas_call(
        _sample_front_body, out_shape=out_shape, scratch_shapes=scratch,
        compiler_params=pltpu.CompilerParams(vmem_limit_bytes=VMEM_LIMIT),
        name="sample_front",
    )(x3, wbig, wgate, wqk, wv, wb, wcre, wcim, wglu, lnmix, convw, convb, bi8, bf8, s5d, bglu, are8, aim8,
      n3, mrow_in, conv_t, sre_t, sim_t)


def _sample_state_body(c_ref, q_ref, k_ref, wlv_ref, colt_ref, qc_ref, cnew_ref):
    sub = lax.broadcasted_iota(jnp.int32, (S_BLK, HEAD_DIM), 0) % 8
    for sb in range(S_SB):
        rows = slice(sb * S_BLK, (sb + 1) * S_BLK)
        for h in range(N_HEADS):
            hs = slice(h * HEAD_DIM, (h + 1) * HEAD_DIM)
            q = q_ref[rows, hs]
            k = k_ref[rows, hs]
            wlv = wlv_ref[rows, hs]
            acc = jnp.zeros((S_BLK, HEAD_DIM), F32)
            for b8 in range(8):
                c_old = c_ref[0, sb * 8 + b8, h]
                sel = sub == b8
                acc = jnp.where(sel, _mm(q, c_old.astype(BF16)), acc)
                upd = lax.dot_general(k, jnp.where(sel, wlv, jnp.zeros_like(wlv)), (((0,), (0,)), ((), ())),
                                      preferred_element_type=F32)
                dl = colt_ref[sb * S_BLK + b8:sb * S_BLK + b8 + 1, COL_DL + h:COL_DL + h + 1]
                cnew_ref[0, sb * 8 + b8, h] = dl * c_old + upd
            qc_ref[rows, hs] = acc


S_SB = 2


def _sample_state_call(c_state, q_bf, k_bf, wlv_bf, colt):
    row_spec = lambda w: pl.BlockSpec((S_SB * S_BLK, w), lambda i: (i, 0))
    c_spec = pl.BlockSpec((1, S_SB * 8, N_HEADS, HEAD_DIM, HEAD_DIM), lambda i: (0, i, 0, 0, 0))
    return pl.pallas_call(
        _sample_state_body,
        grid=(S_RB // S_SB,),
        in_specs=[c_spec, row_spec(MIX), row_spec(MIX), row_spec(MIX), row_spec(128)],
        out_specs=(row_spec(MIX), c_spec),
        out_shape=(jax.ShapeDtypeStruct((ROWS, MIX), F32), jax.ShapeDtypeStruct(c_state.shape, F32)),
        compiler_params=pltpu.CompilerParams(dimension_semantics=("arbitrary",), vmem_limit_bytes=VMEM_LIMIT),
        name="sample_state",
    )(c_state, q_bf, k_bf, wlv_bf, colt)


def _sample_back_body(x_ref, p_ref, num_ref, dec_ref, den_ref, qc_ref, og_ref, zg_ref, xc_ref, yb_ref,
                      wout_ref, wple_ref, wpg_ref, lnhead_ref, skip_ref, lnple_ref, lnfin_ref, y_ref,
                      xtm_s, ptm_s):
    ha = (num_ref[...] + dec_ref[...] * qc_ref[...]) / den_ref[...] * og_ref[...]
    ha = jnp.concatenate([_head_norm(ha[:, h * HEAD_DIM:(h + 1) * HEAD_DIM]) for h in range(N_HEADS)], axis=-1)
    ha = (ha * lnhead_ref[...] + skip_ref[...] * xc_ref[...]) * zg_ref[...]
    y = _merge_out(_rows_to_blocks(x_ref, xtm_s), _rows_to_blocks(p_ref, ptm_s), ha, yb_ref[...],
                   wout_ref, wple_ref, wpg_ref, lnple_ref, lnfin_ref)
    for rb in range(S_RB):
        for t in range(S_T):
            y_ref[rb * 8:(rb + 1) * 8, t, :] = y[rb * S_BLK + t * 8:rb * S_BLK + t * 8 + 8, :]


def _sample_back_call(x3, p3, num, dec, den, qc, og, zg, xc, yb, weights, vecs):
    (_, _, _, _, _, _, _, _, wout, wple, wpg) = weights
    (_, _, _, _, _, lnhead, skip, _, _, lnple, lnfin) = vecs
    return pl.pallas_call(
        _sample_back_body, out_shape=jax.ShapeDtypeStruct(x3.shape, F32),
        scratch_shapes=[pltpu.VMEM((ROWS, D_MODEL), F32), pltpu.VMEM((ROWS, D_PLE), F32)],
        compiler_params=pltpu.CompilerParams(vmem_limit_bytes=VMEM_LIMIT),
        name="sample_back",
    )(x3, p3, num, dec, den, qc, og, zg, xc, yb, wout, wple, wpg, lnhead, skip, lnple, lnfin)


def _sample_path(x_sample, p_sample, c_state, n_state, m_state, conv_state, sre_state, sim_state,
                 weights, vecs, are8, aim8):
    nb, n_t, _ = x_sample.shape
    assert nb == S_RB * 8 and n_t == S_T
    m_rows = jnp.broadcast_to(jnp.transpose(m_state).reshape(N_HEADS, S_RB, 1, 8), (N_HEADS, S_RB, S_T, 8))
    mrow_in = jnp.pad(m_rows.reshape(N_HEADS, ROWS), ((0, 8 - N_HEADS), (0, 0)))
    s5_t = lambda s: jnp.transpose(s, (1, 2, 0)).reshape(N_STATE, nb)
    (q_bf, k_bf, wlv_bf, colt, num, dec, den, og, zg, xc, yb, n_new, mrow, conv_new, sre, sim) = _sample_front_call(
        x_sample, weights, vecs, are8, aim8, n_state, mrow_in, jnp.transpose(conv_state, (1, 0, 2)),
        s5_t(sre_state), s5_t(sim_state))
    qc, c_new = _sample_state_call(c_state, q_bf, k_bf, wlv_bf, colt)
    y_s = _sample_back_call(x_sample, p_sample, num, dec, den, qc, og, zg, xc, yb, weights, vecs)
    m_new = jnp.transpose(mrow[0:N_HEADS].reshape(N_HEADS, S_RB, S_T, 8)[:, :, 0, :].reshape(N_HEADS, nb))
    s5_back = lambda s: jnp.transpose(s.reshape(N_GROUPS, S5_STATE, nb), (2, 0, 1))[None]
    return (y_s, c_new, n_new[None], m_new[None], jnp.transpose(conv_new, (1, 0, 2))[None],
            s5_back(sre), s5_back(sim))


IN_SPLITS = (0, MIX, 2 * MIX, 3 * MIX + 2 * N_HEADS, 4 * MIX + 2 * N_HEADS)
GATE_ROW = 3 * MIX


def _weight_prep_body(wint_ref, wq_ref, wk_ref, wv_ref, wglu_ref, wout_ref, wple_ref, wpg_ref,
                      wbig_ref, wgate_ref, wqk_ref, wvo_ref, wgluo_ref, wouto_ref, wpleo_ref, wpgo_ref):
    for s, r0 in enumerate(IN_SPLITS):
        wbig_ref[:, s * MIX:(s + 1) * MIX] = wint_ref[r0:r0 + MIX, :].T.astype(BF16)
    g8 = wint_ref[GATE_ROW:GATE_ROW + 8, :]
    row = lax.broadcasted_iota(jnp.int32, g8.shape, 0)
    zero = jnp.zeros_like(g8)
    top = jnp.where(row < N_HEADS, g8, zero)
    bot = jnp.where(row < N_HEADS, pltpu.roll(g8, N_HEADS, axis=0), zero)
    wgate_ref[...] = jnp.concatenate([top, bot], axis=0).astype(BF16)
    for h in range(N_HEADS):
        wqk_ref[h, :, 0:HEAD_DIM] = wq_ref[0, h].astype(BF16)
        wqk_ref[h, :, HEAD_DIM:2 * HEAD_DIM] = wk_ref[0, h].astype(BF16)
    wvo_ref[...] = wv_ref[0].astype(BF16)
    wgluo_ref[...] = wglu_ref[0].astype(BF16)
    wouto_ref[...] = wout_ref[0].astype(BF16)
    wpleo_ref[...] = wple_ref[0].astype(BF16)
    wpgo_ref[...] = wpg_ref[0].astype(BF16)


def _weight_prep_call(w_in_t, w_q, w_k, w_v, w_glu, w_out, w_ple, w_pg):
    bf = lambda *s: jax.ShapeDtypeStruct(s, BF16)
    return pl.pallas_call(
        _weight_prep_body,
        out_shape=(bf(D_MODEL, 5 * MIX), bf(16, D_MODEL), bf(N_HEADS, HEAD_DIM, 2 * HEAD_DIM),
                   bf(N_HEADS, HEAD_DIM, HEAD_DIM), bf(MIX, MIX), bf(2 * MIX, D_MODEL), bf(D_PLE, D_MODEL),
                   bf(D_MODEL, D_MODEL)),
        compiler_params=pltpu.CompilerParams(vmem_limit_bytes=VMEM_LIMIT),
        name="weight_prep",
    )(w_in_t, w_q, w_k, w_v, w_glu, w_out, w_ple, w_pg)


def _layout_params(ln_mix, w_in, b_igate, b_fgate, conv_w, conv_b, w_q, w_k, w_v, ln_head, skip_a,
                   s5_lam_re, s5_lam_im, s5_log_dt, s5_B_re, s5_B_im, s5_C_re, s5_C_im, s5_D, w_glu, b_glu,
                   w_out, w_ple, ln_ple, w_ple_gate, ln_final):
    wbig, wgate, wqk, wv, wglu_bf, wout_bf, wple_bf, wpg_bf = _weight_prep_call(
        jnp.transpose(w_in[0]), w_q, w_k, w_v, w_glu, w_out, w_ple, w_ple_gate)
    b_gcp = lambda m: jnp.transpose(m[0], (0, 2, 1)).reshape(2, 16 * S5_GROUP, S5_STATE)
    c_gcp = lambda m: m[0].reshape(2, 16 * S5_GROUP, S5_STATE)
    lam = jnp.concatenate([v.reshape(1, N_STATE) for v in (s5_lam_re, s5_lam_im, s5_log_dt)], axis=0)
    are8, aim8, wb, wcre, wcim = _prep_call(lam, b_gcp(s5_B_re), b_gcp(s5_B_im), c_gcp(s5_C_re), c_gcp(s5_C_im))
    weights = [wbig, wgate, wqk, wv, wb, wcre, wcim, wglu_bf, wout_bf, wple_bf, wpg_bf]
    pad8 = lambda v: jnp.pad(v[0][:, None], ((0, 8 - N_HEADS), (0, 0)))
    vecs = [ln_mix[0][None, :], conv_w[0], conv_b[0][None, :], pad8(b_igate), pad8(b_fgate),
            ln_head[0][None, :], skip_a[0][None, :], s5_D[0][None, :], b_glu[0][None, :],
            ln_ple[0][None, :], ln_final[None, :]]
    return weights, vecs, are8, aim8


def kernel(x_prompt, x_sample, p_prompt, p_sample, state_mlstm_C, state_mlstm_n, state_mlstm_m, state_conv, state_s5_re, state_s5_im, ln_mix, w_in, b_igate, b_fgate, conv_w, conv_b, w_q, w_k, w_v, ln_head, skip_a, s5_lam_re, s5_lam_im, s5_log_dt, s5_B_re, s5_B_im, s5_C_re, s5_C_im, s5_D, w_glu, b_glu, w_out, w_ple, ln_ple, w_ple_gate, ln_final):
    weights, vecs, are8, aim8 = _layout_params(
        ln_mix, w_in, b_igate, b_fgate, conv_w, conv_b, w_q, w_k, w_v, ln_head, skip_a, s5_lam_re, s5_lam_im,
        s5_log_dt, s5_B_re, s5_B_im, s5_C_re, s5_C_im, s5_D, w_glu, b_glu, w_out, w_ple, ln_ple, w_ple_gate,
        ln_final)

    nbp = x_prompt.shape[0]
    y_p, c_p, n_p, mrow_p, conv_t, sre_p, sim_p = _prompt_call(x_prompt, p_prompt[0], weights, vecs, are8, aim8)
    conv_p = jnp.transpose(conv_t, (1, 0, 2))[None]
    pn = n_p.reshape(1, nbp, N_HEADS, HEAD_DIM)
    pm = jnp.transpose(mrow_p[0:N_HEADS, ::P_T])[None]
    pre = sre_p.reshape(1, nbp, N_GROUPS, S5_STATE)
    pim = sim_p.reshape(1, nbp, N_GROUPS, S5_STATE)

    y_s, c_s, n_s, m_s, conv_s, sre_s, sim_s = _sample_path(
        x_sample, p_sample[0], state_mlstm_C, state_mlstm_n[0], state_mlstm_m[0], state_conv[0],
        state_s5_re[0], state_s5_im[0], weights, vecs, are8, aim8)
    return (y_p, y_s, c_p, pn, pm, conv_p, pre, pim, c_s, n_s, m_s, conv_s, sre_s, sim_s)
```

```python
import functools
import math

import jax
import jax.numpy as jnp
from jax import lax
from jax.experimental import pallas as pl
from jax.experimental.pallas import tpu as pltpu

F32 = jnp.float32
BF16 = jnp.bfloat16

D_MODEL = 1024
D_PLE = 256
MIX = 512
N_HEADS = 4
HEAD_DIM = 128
N_GROUPS = 32
S5_GROUP = 16
S5_STATE = 64
N_STATE = N_GROUPS * S5_STATE
CONV_W = 4
EPS = 1e-6
K_SCALE = HEAD_DIM ** -0.5
ROWS = 512
GROUP_ROWS = 256
VMEM_LIMIT = 60 * 1024 * 1024


def _mm(a, w):
    return jnp.dot(a, w, preferred_element_type=F32)


def _rms(x, g):
    ms = jnp.mean(x * x, axis=-1, keepdims=True)
    return x * lax.rsqrt(ms + EPS) * g


def _sigmoid(x):
    return 0.5 * jnp.tanh(0.5 * x) + 0.5


def _silu(x):
    h = 0.5 * x
    return h + h * jnp.tanh(h)


def _gelu_tanh(x):
    c = math.sqrt(2.0 / math.pi)
    return 0.5 * x * (1.0 + jnp.tanh(c * (x + 0.044715 * (x * x * x))))


def _log_sigmoid(x):
    return jnp.minimum(x, 0.0) - jnp.log1p(jnp.exp(-jnp.abs(x)))


SCAN_RADIX = 8


def _seg_scan(x, tidx, n_t, t_stride, op, fill):
    span = 1
    while span < n_t:
        acc = x
        for j in range(1, SCAN_RADIX):
            if j * span >= n_t:
                break
            prev = pltpu.roll(x, j * span * t_stride, axis=1)
            acc = op(acc, jnp.where(tidx >= j * span, prev, fill))
        x = acc
        span *= SCAN_RADIX
    return x


def _bcast_last(x, tidx, n_t, t_stride):
    y = jnp.where(tidx == n_t - 1, x, 0.0)
    span = 1
    while span < n_t:
        acc = y
        for j in range(1, SCAN_RADIX):
            if j * span >= n_t:
                break
            nxt = pltpu.roll(y, ROWS - j * span * t_stride, axis=1)
            acc = acc + jnp.where(tidx + j * span <= n_t - 1, nxt, 0.0)
        y = acc
        span *= SCAN_RADIX
    return y


def _gate_rows(a, wg_ref, bi8, bf8, mprev, n_t, t_stride):
    gt = lax.dot_general(wg_ref[...], a, (((1,), (1,)), ((), ())), preferred_element_type=F32)
    lane = lax.broadcasted_iota(jnp.int32, (8, ROWS), 1)
    tidx = (lane // t_stride) % n_t
    i8 = gt[0:8] + bi8
    logf = _log_sigmoid(gt[8:16] + bf8)
    b = _seg_scan(logf, tidx, n_t, t_stride, jnp.add, 0.0)
    c = i8 - b
    cm = _seg_scan(c, tidx, n_t, t_stride, jnp.maximum, -jnp.inf)
    big_m = jnp.maximum(mprev, cm)
    mt = b + big_m
    m_last = _bcast_last(big_m, tidx, n_t, t_stride)
    mnew = _bcast_last(mt, tidx, n_t, t_stride)
    stack = jnp.concatenate(
        [-big_m, jnp.exp(mprev - big_m), jnp.exp(-mt), jnp.exp(c - m_last), jnp.exp(mprev - m_last),
         jnp.zeros((128 - 40, ROWS), F32)], axis=0)
    return c, stack.T, mnew


COL_NEGM, COL_DECAY, COL_EM, COL_WL, COL_DL = 0, 8, 16, 24, 32


def _intra(qg, kg, vg, col_a, row_b, mask):
    s = lax.dot_general(qg, kg, (((1,), (1,)), ((), ())), preferred_element_type=F32)
    p = s * jnp.exp(jnp.where(mask, col_a + row_b, -jnp.inf))
    rs = jnp.sum(p, axis=-1, keepdims=True)
    return _mm(p.astype(BF16), vg), rs


def _head_norm(x):
    mu = jnp.mean(x, axis=-1, keepdims=True)
    xc = x - mu
    var = jnp.mean(xc * xc, axis=-1, keepdims=True)
    return xc * lax.rsqrt(var + EPS)


S5_HALF = N_STATE // 2


S5_CHUNK = 512
S5_N_CHUNKS = N_STATE // S5_CHUNK


def _s5_scan(views, are_ref, aim_ref, sre_ref, sim_ref, row_base, st_rows, n_t, chunks=range(S5_N_CHUNKS)):
    chunk = S5_CHUNK
    assert n_t % 2 == 0 and row_base % 16 == 0
    for gch in chunks:
        hf, ch = divmod(gch, S5_HALF // chunk)
        bure_s, buim_s, xsre_s, xsim_s = views[hf]
        ls = slice(ch * chunk, (ch + 1) * chunk)
        gl = slice(hf * S5_HALF + ch * chunk, hf * S5_HALF + (ch + 1) * chunk)
        ar = are_ref[:, gl]
        ai = aim_ref[:, gl]
        xr = sre_ref[st_rows, gl]
        xi = sim_ref[st_rows, gl]
        for t in range(0, n_t, 2):
            r0 = row_base + t * 8
            xr1, xi1 = (ar * xr - ai * xi + bure_s[r0:r0 + 8, ls], ar * xi + ai * xr + buim_s[r0:r0 + 8, ls])
            xr, xi = (ar * xr1 - ai * xi1 + bure_s[r0 + 8:r0 + 16, ls],
                      ar * xi1 + ai * xr1 + buim_s[r0 + 8:r0 + 16, ls])
            xsre_s[r0:r0 + 16, ls] = jnp.concatenate([xr1, xr], axis=0).astype(BF16)
            xsim_s[r0:r0 + 16, ls] = jnp.concatenate([xi1, xi], axis=0).astype(BF16)
        sre_ref[st_rows, gl] = xr
        sim_ref[st_rows, gl] = xi


def _s5_views(bure_s, buim_s, xsre_s, xsim_s):
    return [[r.at[hf] for r in (bure_s, buim_s, xsre_s, xsim_s)] for hf in range(2)]


def _s5_in(views, u_tm, wb_ref):
    for hf in range(2):
        res = _mm(u_tm[:, hf * 256:(hf + 1) * 256].astype(BF16), wb_ref[hf])
        views[hf][0][...] = res[:, :S5_HALF]
        views[hf][1][...] = res[:, S5_HALF:]


def _s5_out(views, wcre_ref, wcim_ref):
    return [_mm(views[hf][2][...], wcre_ref[hf]) - _mm(views[hf][3][...], wcim_ref[hf]) for hf in range(2)]


def _s5_post(y_s5, u, zs, wglu_ref, s5d_ref, bglu_ref):
    yb = _gelu_tanh(y_s5 + s5d_ref[...] * u)
    yb = yb * _sigmoid(_mm(yb.astype(BF16), wglu_ref[...]) + bglu_ref[...])
    return yb * _silu(zs)


OUT_BLOCK = 256


def _merge_out(x, e_raw, h_a, y_b, wout_ref, wpg_ref, lnple_ref, lnfin_ref):
    h1 = x + _mm(h_a.astype(BF16), wout_ref[0:MIX, :]) + _mm(y_b.astype(BF16), wout_ref[MIX:2 * MIX, :])
    e_half = 0.5 * _rms(e_raw, lnple_ref[...])
    h1_bf = h1.astype(BF16)
    base = h1 + e_half
    blocks = []
    ssq = jnp.zeros((h1.shape[0], 1), F32)
    for nb in range(D_MODEL // OUT_BLOCK):
        cs = slice(nb * OUT_BLOCK, (nb + 1) * OUT_BLOCK)
        h2 = base[:, cs] + e_half[:, cs] * jnp.tanh(0.5 * _mm(h1_bf, wpg_ref[:, cs]))
        ssq = ssq + jnp.sum(h2 * h2, axis=-1, keepdims=True)
        blocks.append(h2)
    scale = lax.rsqrt(ssq * (1.0 / D_MODEL) + EPS)
    return jnp.concatenate(blocks, axis=-1) * scale * lnfin_ref[...]


def _prep_body(lam_ref, btr_ref, bti_ref, ctr_ref, cti_ref, are_ref, aim_ref, wb_ref, wcre_ref, wcim_ref):
    lr = jnp.minimum(lam_ref[0:1, :], -1e-4)
    li = lam_ref[1:2, :]
    dt = jnp.exp(lam_ref[2:3, :])
    mag = jnp.exp(lr * dt)
    a_re = mag * jnp.cos(li * dt)
    a_im = mag * jnp.sin(li * dt)
    den = lr * lr + li * li
    xr = a_re - 1.0
    g_re = (xr * lr + a_im * li) / den
    g_im = (a_im * lr - xr * li) / den
    are_ref[...] = jnp.broadcast_to(a_re, (8, N_STATE))
    aim_ref[...] = jnp.broadcast_to(a_im, (8, N_STATE))
    half = N_STATE // 2
    own_block = ((lax.broadcasted_iota(jnp.int32, (256, half), 0) // S5_GROUP)
                 == (lax.broadcasted_iota(jnp.int32, (256, half), 1) // S5_STATE))
    blocks = lambda ref, hf: jnp.where(own_block, jnp.concatenate([ref[hf]] * 16, axis=1), 0.0)
    for hf in range(2):
        gr = g_re[:, hf * half:(hf + 1) * half]
        gi = g_im[:, hf * half:(hf + 1) * half]
        br = blocks(btr_ref, hf)
        bi = blocks(bti_ref, hf)
        wb_ref[hf, :, 0:half] = (gr * br - gi * bi).astype(BF16)
        wb_ref[hf, :, half:2 * half] = (gr * bi + gi * br).astype(BF16)
        wcre_ref[hf] = blocks(ctr_ref, hf).T.astype(BF16)
        wcim_ref[hf] = blocks(cti_ref, hf).T.astype(BF16)


def _prep_call(lam, bt_re, bt_im, ct_re, ct_im):
    half = N_STATE // 2
    return pl.pallas_call(
        _prep_body,
        out_shape=(jax.ShapeDtypeStruct((8, N_STATE), F32), jax.ShapeDtypeStruct((8, N_STATE), F32),
                   jax.ShapeDtypeStruct((2, 256, N_STATE), BF16),
                   jax.ShapeDtypeStruct((2, half, 256), BF16), jax.ShapeDtypeStruct((2, half, 256), BF16)),
        compiler_params=pltpu.CompilerParams(vmem_limit_bytes=VMEM_LIMIT),
        name="s5_prep",
    )(lam, bt_re, bt_im, ct_re, ct_im)


P_T = 64
P_NB = 8


P_SUB = 2


def _prompt_body(*refs):
    n_in = 26
    c_ref, n_ref, mrow_ref = refs[n_in + 1:n_in + 4]
    sre_ref, sim_ref, ext_s = refs[n_in + 5:n_in + 8]
    assert len(refs) == n_in + 7 + 9

    @pl.when(pl.program_id(0) == 0)
    def _init():
        c_ref[...] = jnp.zeros_like(c_ref)
        n_ref[...] = jnp.zeros_like(n_ref)
        mrow_ref[...] = jnp.zeros_like(mrow_ref)
        sre_ref[...] = jnp.zeros_like(sre_ref)
        sim_ref[...] = jnp.zeros_like(sim_ref)
        ext_s[:, 0:8, :] = jnp.zeros((P_NB, 8, MIX), F32)

    def one_tile(sub, carry):
        _prompt_tile(pl.ds(pl.multiple_of(sub * P_T, P_T), P_T), *refs)
        return carry

    lax.fori_loop(0, P_SUB, one_tile, 0)


def _prompt_tile(ts, x_ref, p_ref, wbig_ref, wgate_ref, wqk_ref, wv_ref, wb_ref, wcre_ref, wcim_ref,
                 wglu_ref, wout_ref, wple_ref, wpg_ref, lnmix_ref, convw_ref, convb_ref, bi8_ref, bf8_ref,
                 lnhead_ref, skip_ref, s5d_ref, bglu_ref, lnple_ref, lnfin_ref, are_ref, aim_ref,
                 y_ref, c_ref, n_ref, mrow_ref, conv_ref, sre_ref, sim_ref,
                 ext_s, ha_s, utm_s, ytm_s, ybm_s, bure_s, buim_s, xsre_s, xsim_s):
    x = x_ref[:, ts, :].reshape(ROWS, D_MODEL)
    a = _rms(x, lnmix_ref[...]).astype(BF16)

    pair_w = 2 * HEAD_DIM
    xc_parts, xc_bf_parts, xm_bf_parts = [], [], []
    xms = []
    for pr in range(N_HEADS // 2):
        ps = slice(pr * pair_w, (pr + 1) * pair_w)
        xm = _mm(a, wbig_ref[:, ps])
        ext_s[:, 8:8 + P_T, ps] = xm.reshape(P_NB, P_T, pair_w)
        xms.append(xm)

    c8, colt, mnew = _gate_rows(a, wgate_ref, bi8_ref[...], bf8_ref[...], mrow_ref[...], P_T, 1)
    mrow_ref[...] = mnew

    for pr in range(N_HEADS // 2):
        ps = slice(pr * pair_w, (pr + 1) * pair_w)
        xm = xms[pr]
        yc = convb_ref[:, ps]
        for j in range(CONV_W):
            yc = yc + ext_s[:, 5 + j:5 + j + P_T, ps].reshape(ROWS, pair_w) * convw_ref[j:j + 1, ps]
        for j in range(CONV_W - 1):
            conv_ref[j, :, ps] = ext_s[:, 5 + P_T + j, ps]
        ext_s[:, 5:8, ps] = ext_s[:, 5 + P_T:8 + P_T, ps]
        xc_pr = _silu(yc)
        xc_parts.append(xc_pr)
        xc_bf_parts.append(xc_pr.astype(BF16))
        xm_bf_parts.append(xm.astype(BF16))
    xc = jnp.concatenate(xc_parts, axis=-1)

    zs_raw = _mm(a, wbig_ref[:, 2048:2560])
    e_raw = _mm(p_ref[:, ts, :].reshape(ROWS, D_PLE).astype(BF16), wple_ref[...])

    views = _s5_views(bure_s, buim_s, xsre_s, xsim_s)

    def s5_front():
        u = _mm(a, wbig_ref[:, 1536:2048])
        for b in range(P_NB):
            for j in range(P_T // 8):
                for lc in range(MIX // 128):
                    utm_s[lc, pl.ds(j * 64 + b, 8, stride=8), :] = (
                        u[b * P_T + j * 8:b * P_T + j * 8 + 8, lc * 128:(lc + 1) * 128])
        _s5_in(views, jnp.concatenate([utm_s[lc] for lc in range(MIX // 128)], axis=-1), wb_ref)
        _s5_scan(views, are_ref, aim_ref, sre_ref, sim_ref, 0, slice(0, 8), P_T)
        return u

    def s5_back(u):
        y_halves = _s5_out(views, wcre_ref, wcim_ref)
        for hf, y_half in enumerate(y_halves):
            ytm_s[2 * hf] = y_half[:, 0:128]
            ytm_s[2 * hf + 1] = y_half[:, 128:256]
        for b in range(P_NB):
            for j in range(P_T // 8):
                for lc in range(MIX // 128):
                    ybm_s[b * P_T + j * 8:b * P_T + j * 8 + 8, lc * 128:(lc + 1) * 128] = (
                        ytm_s[lc, pl.ds(j * 64 + b, 8, stride=8), :])
        return _s5_post(ybm_s[...], u, zs_raw, wglu_ref, s5d_ref, bglu_ref)

    r = lax.broadcasted_iota(jnp.int32, (GROUP_ROWS, GROUP_ROWS), 0)
    cc = lax.broadcasted_iota(jnp.int32, (GROUP_ROWS, GROUP_ROWS), 1)
    mask = ((r // P_T) == (cc // P_T)) & (cc <= r)

    for h in range(N_HEADS):
        hs = slice(h * HEAD_DIM, (h + 1) * HEAD_DIM)
        hp = slice((h % 2) * HEAD_DIM, (h % 2 + 1) * HEAD_DIM)
        qk = _mm(xc_bf_parts[h // 2][:, hp], wqk_ref[h])
        q_f = qk[:, :HEAD_DIM]
        k_f = qk[:, HEAD_DIM:] * K_SCALE
        v_f = _mm(xm_bf_parts[h // 2][:, hp], wv_ref[h])
        q_bf = q_f.astype(BF16)
        k_bf = k_f.astype(BF16)
        v_bf = v_f.astype(BF16)
        for g in range(ROWS // GROUP_ROWS):
            gs = slice(g * GROUP_ROWS, (g + 1) * GROUP_ROWS)
            decay = colt[gs, COL_DECAY + h:COL_DECAY + h + 1]
            wl = colt[gs, COL_WL + h:COL_WL + h + 1]
            dl = colt[gs, COL_DL + h:COL_DL + h + 1]
            num, rs = _intra(q_bf[gs], k_bf[gs], v_bf[gs], colt[gs, COL_NEGM + h:COL_NEGM + h + 1],
                             c8[h:h + 1, gs], mask)
            wlv = (wl * v_f[gs]).astype(BF16)
            wlk = wl * k_f[gs]
            qcs, qns = [], []
            for bb in range(GROUP_ROWS // P_T):
                b = g * (GROUP_ROWS // P_T) + bb
                rb = slice(g * GROUP_ROWS + bb * P_T, g * GROUP_ROWS + (bb + 1) * P_T)
                rl = slice(bb * P_T, (bb + 1) * P_T)
                c_old = c_ref[0, b, h]
                n_old = n_ref[b:b + 1, hs]
                qcs.append(_mm(q_bf[rb], c_old.astype(BF16)))
                qns.append(jnp.sum(q_f[rb] * n_old, axis=-1, keepdims=True))
                upd = lax.dot_general(k_bf[rb], wlv[rl], (((0,), (0,)), ((), ())), preferred_element_type=F32)
                dlb = dl[rl]
                c_ref[0, b, h] = jnp.concatenate([dlb, dlb], axis=0) * c_old + upd
                n_ref[b:b + 1, hs] = dlb[0:1] * n_old + jnp.sum(wlk[rl], axis=0, keepdims=True)
            num = num + decay * jnp.concatenate(qcs, axis=0)
            qn = rs + decay * jnp.concatenate(qns, axis=0)
            den = jnp.maximum(jnp.abs(qn), colt[gs, COL_EM + h:COL_EM + h + 1])
            ha_s[gs, hs] = num / den

    yb = s5_back(s5_front())
    ha = ha_s[...] * _sigmoid(_mm(a, wbig_ref[:, 1024:1536]))
    ha = jnp.concatenate([_head_norm(ha[:, h * HEAD_DIM:(h + 1) * HEAD_DIM]) for h in range(N_HEADS)], axis=-1)
    ha = (ha * lnhead_ref[...] + skip_ref[...] * xc) * _silu(_mm(a, wbig_ref[:, 512:1024]))

    y = _merge_out(x, e_raw, ha, yb, wout_ref, wpg_ref, lnple_ref, lnfin_ref)
    y_ref[:, ts, :] = y.reshape(P_NB, P_T, D_MODEL)


def _const_spec(shape):
    nd = len(shape)
    return pl.BlockSpec(shape, lambda i, nd=nd: (0,) * nd, pipeline_mode=pl.Buffered(1))


def _prompt_call(x, p, weights, vecs, are8, aim8):
    nb, seq, _ = x.shape
    blk_t = P_T * P_SUB
    assert nb == P_NB and seq % blk_t == 0
    n_steps = seq // blk_t
    w_specs = [_const_spec(w.shape) for w in weights]
    v_specs = [_const_spec(v.shape) for v in vecs]
    in_specs = ([pl.BlockSpec((P_NB, blk_t, D_MODEL), lambda i: (0, i, 0)),
                 pl.BlockSpec((P_NB, blk_t, D_PLE), lambda i: (0, i, 0))]
                + w_specs + v_specs + [_const_spec((8, N_STATE)), _const_spec((8, N_STATE))])
    out_shape = (
        jax.ShapeDtypeStruct((nb, seq, D_MODEL), F32),
        jax.ShapeDtypeStruct((1, nb, N_HEADS, HEAD_DIM, HEAD_DIM), F32),
        jax.ShapeDtypeStruct((nb, MIX), F32),
        jax.ShapeDtypeStruct((8, ROWS), F32),
        jax.ShapeDtypeStruct((CONV_W - 1, nb, MIX), F32),
        jax.ShapeDtypeStruct((8, N_STATE), F32),
        jax.ShapeDtypeStruct((8, N_STATE), F32),
    )
    out_specs = (
        pl.BlockSpec((P_NB, blk_t, D_MODEL), lambda i: (0, i, 0)),
        pl.BlockSpec((1, nb, N_HEADS, HEAD_DIM, HEAD_DIM), lambda i: (0, 0, 0, 0, 0)),
        pl.BlockSpec((nb, MIX), lambda i: (0, 0)),
        pl.BlockSpec((8, ROWS), lambda i: (0, 0)),
        pl.BlockSpec((CONV_W - 1, nb, MIX), lambda i: (0, 0, 0)),
        pl.BlockSpec((8, N_STATE), lambda i: (0, 0)),
        pl.BlockSpec((8, N_STATE), lambda i: (0, 0)),
    )
    scratch = [
        pltpu.VMEM((P_NB, P_T + 8, MIX), F32),
        pltpu.VMEM((ROWS, MIX), F32),
        pltpu.VMEM((MIX // 128, ROWS, 128), F32),
        pltpu.VMEM((MIX // 128, ROWS, 128), F32),
        pltpu.VMEM((ROWS, MIX), F32),
        pltpu.VMEM((2, ROWS, S5_HALF), F32),
        pltpu.VMEM((2, ROWS, S5_HALF), F32),
        pltpu.VMEM((2, ROWS, S5_HALF), BF16),
        pltpu.VMEM((2, ROWS, S5_HALF), BF16),
    ]
    return pl.pallas_call(
        _prompt_body,
        grid=(n_steps,),
        in_specs=in_specs,
        out_specs=out_specs,
        out_shape=out_shape,
        scratch_shapes=scratch,
        compiler_params=pltpu.CompilerParams(dimension_semantics=("arbitrary",), vmem_limit_bytes=VMEM_LIMIT),
        name="prompt_layer",
    )(x, p, *weights, *vecs, are8, aim8)


S_T = 4
S_RB = ROWS // (8 * S_T)
S_BLK = 8 * S_T


def _sample_front_body(x_ref, wbig_ref, wgate_ref, wqk_ref, wv_ref, wb_ref, wcre_ref, wcim_ref, wglu_ref,
                       lnmix_ref, convw_ref, convb_ref, bi8_ref, bf8_ref, s5d_ref, bglu_ref, are_ref, aim_ref,
                       n_in_ref, mrow_in_ref, conv_in_ref, sre_in_ref, sim_in_ref,
                       q_ref, k_ref, wlv_ref, colt_ref, num_ref, dec_ref, den_ref, og_ref, zg_ref, xc_ref, yb_ref,
                       n_ref, mrow_ref, conv_ref, sre_out_ref, sim_out_ref,
                       ext_s, bure_s, buim_s, xsre_s, xsim_s, xtm_s, sre_ref, sim_ref):
    x = _rows_to_blocks(x_ref, xtm_s)
    a = _rms(x, lnmix_ref[...]).astype(BF16)

    xm = _mm(a, wbig_ref[:, 0:512])
    ext_rows = (CONV_W - 1 + S_T) * 8
    ycs = []
    for rb in range(S_RB):
        seqs = slice(rb * 8, (rb + 1) * 8)
        for j in range(CONV_W - 1):
            ext_s[rb * ext_rows + 8 * j:rb * ext_rows + 8 * j + 8, :] = conv_in_ref[j, seqs, :]
        ext_s[rb * ext_rows + 24:(rb + 1) * ext_rows, :] = xm[rb * S_BLK:(rb + 1) * S_BLK, :]
        yc = convb_ref[...]
        for j in range(CONV_W):
            yc = yc + ext_s[rb * ext_rows + 8 * j:rb * ext_rows + 8 * j + S_BLK, :] * convw_ref[j:j + 1, :]
        ycs.append(yc)
        for j in range(CONV_W - 1):
            conv_ref[j, seqs, :] = ext_s[rb * ext_rows + S_BLK + 8 * j:rb * ext_rows + S_BLK + 8 * j + 8, :]
    xc = _silu(jnp.concatenate(ycs, axis=0))
    xc_ref[...] = xc
    xc_bf = xc.astype(BF16)
    xm_bf = xm.astype(BF16)

    mprev = mrow_in_ref[...]
    c8, colt, mnew = _gate_rows(a, wgate_ref, bi8_ref[...], bf8_ref[...], mprev, S_T, 8)
    mrow_ref[...] = mnew
    colt_ref[...] = colt

    r = lax.broadcasted_iota(jnp.int32, (GROUP_ROWS, GROUP_ROWS), 0)
    cc = lax.broadcasted_iota(jnp.int32, (GROUP_ROWS, GROUP_ROWS), 1)
    same_seq = ((r // S_BLK) == (cc // S_BLK)) & ((r % 8) == (cc % 8))
    mask = same_seq & (((cc // 8) % S_T) <= ((r // 8) % S_T))

    nb = S_RB * 8
    for h in range(N_HEADS):
        hs = slice(h * HEAD_DIM, (h + 1) * HEAD_DIM)
        qk = _mm(xc_bf[:, hs], wqk_ref[h])
        q_f = qk[:, :HEAD_DIM]
        k_f = qk[:, HEAD_DIM:] * K_SCALE
        v_f = _mm(xm_bf[:, hs], wv_ref[h])
        q_bf = q_f.astype(BF16)
        k_bf = k_f.astype(BF16)
        v_bf = v_f.astype(BF16)
        decay = colt[:, COL_DECAY + h:COL_DECAY + h + 1]
        wl = colt[:, COL_WL + h:COL_WL + h + 1]
        dl = colt[:, COL_DL + h:COL_DL + h + 1]
        nums, rss = [], []
        for g in range(ROWS // GROUP_ROWS):
            gs = slice(g * GROUP_ROWS, (g + 1) * GROUP_ROWS)
            num, rs = _intra(q_bf[gs], k_bf[gs], v_bf[gs], colt[gs, COL_NEGM + h:COL_NEGM + h + 1],
                             c8[h:h + 1, gs], mask)
            nums.append(num)
            rss.append(rs)
        q_ref[:, hs] = q_bf
        k_ref[:, hs] = k_bf
        wlv_ref[:, hs] = (wl * v_f).astype(BF16)
        num_ref[:, hs] = jnp.concatenate(nums, axis=0)
        n_old = n_in_ref[:, h, :]
        n_rows = jnp.broadcast_to(n_old.reshape(S_RB, 1, 8, HEAD_DIM), (S_RB, S_T, 8, HEAD_DIM)).reshape(ROWS, HEAD_DIM)
        qn = jnp.concatenate(rss, axis=0) + decay * jnp.sum(q_f * n_rows, axis=-1, keepdims=True)
        den = jnp.maximum(jnp.abs(qn), colt[:, COL_EM + h:COL_EM + h + 1])
        dec_ref[:, hs] = jnp.broadcast_to(decay, (ROWS, HEAD_DIM))
        den_ref[:, hs] = jnp.broadcast_to(den, (ROWS, HEAD_DIM))
        wlk_sum = jnp.sum((wl * k_f).reshape(S_RB, S_T, 8, HEAD_DIM), axis=1).reshape(nb, HEAD_DIM)
        dl_seq = dl.reshape(S_RB, S_T, 8, 1)[:, 0].reshape(nb, 1)
        n_ref[:, h, :] = dl_seq * n_old + wlk_sum

    og_ref[...] = _sigmoid(_mm(a, wbig_ref[:, 1024:1536]))
    zg_ref[...] = _silu(_mm(a, wbig_ref[:, 512:1024]))

    u = _mm(a, wbig_ref[:, 1536:2048])
    sre_ref[...] = sre_in_ref[...].T
    sim_ref[...] = sim_in_ref[...].T
    views = _s5_views(bure_s, buim_s, xsre_s, xsim_s)
    _s5_in(views, u, wb_ref)
    for rb in range(S_RB):
        _s5_scan(views, are_ref, aim_ref, sre_ref, sim_ref, rb * S_BLK, slice(rb * 8, (rb + 1) * 8), S_T)
    y_s5 = jnp.concatenate(_s5_out(views, wcre_ref, wcim_ref), axis=-1)
    sre_out_ref[...] = sre_ref[...].T
    sim_out_ref[...] = sim_ref[...].T
    yb_ref[...] = _s5_post(y_s5, u, _mm(a, wbig_ref[:, 2048:2560]), wglu_ref, s5d_ref, bglu_ref)


def _rows_to_blocks(src_ref, dst_s):
    for rb in range(S_RB):
        for t in range(S_T):
            dst_s[rb * S_BLK + t * 8:rb * S_BLK + t * 8 + 8, :] = src_ref[rb * 8:(rb + 1) * 8, t, :]
    return dst_s[...]


def _sample_front_call(x3, weights, vecs, are8, aim8, n3, mrow_in, conv_t, sre_t, sim_t):
    (wbig, wgate, wqk, wv, wb, wcre, wcim, wglu, _, _, _) = weights
    (lnmix, convw, convb, bi8, bf8, _, _, s5d, bglu, _, _) = vecs
    nb = S_RB * 8
    f32 = lambda *s: jax.ShapeDtypeStruct(s, F32)
    bf = lambda *s: jax.ShapeDtypeStruct(s, BF16)
    out_shape = (bf(ROWS, MIX), bf(ROWS, MIX), bf(ROWS, MIX), f32(ROWS, 128), f32(ROWS, MIX), f32(ROWS, MIX),
                 f32(ROWS, MIX), f32(ROWS, MIX), f32(ROWS, MIX), f32(ROWS, MIX), f32(ROWS, MIX),
                 f32(nb, N_HEADS, HEAD_DIM), f32(8, ROWS), f32(CONV_W - 1, nb, MIX), f32(N_STATE, nb),
                 f32(N_STATE, nb))
    scratch = [pltpu.VMEM((S_RB * (CONV_W - 1 + S_T) * 8, MIX), F32),
               pltpu.VMEM((2, ROWS, S5_HALF), F32), pltpu.VMEM((2, ROWS, S5_HALF), F32),
               pltpu.VMEM((2, ROWS, S5_HALF), BF16), pltpu.VMEM((2, ROWS, S5_HALF), BF16),
               pltpu.VMEM((ROWS, D_MODEL), F32), pltpu.VMEM((nb, N_STATE), F32), pltpu.VMEM((nb, N_STATE), F32)]
    return pl.pallas_call(
        _sample_front_body, out_shape=out_shape, scratch_shapes=scratch,
        compiler_params=pltpu.CompilerParams(vmem_limit_bytes=VMEM_LIMIT),
        name="sample_front",
    )(x3, wbig, wgate, wqk, wv, wb, wcre, wcim, wglu, lnmix, convw, convb, bi8, bf8, s5d, bglu, are8, aim8,
      n3, mrow_in, conv_t, sre_t, sim_t)


def _sample_state_body(c_ref, q_ref, k_ref, wlv_ref, colt_ref, qc_ref, cnew_ref):
    sub = lax.broadcasted_iota(jnp.int32, (S_BLK, HEAD_DIM), 0) % 8
    for sb in range(S_SB):
        rows = slice(sb * S_BLK, (sb + 1) * S_BLK)
        for h in range(N_HEADS):
            hs = slice(h * HEAD_DIM, (h + 1) * HEAD_DIM)
            q = q_ref[rows, hs]
            k = k_ref[rows, hs]
            wlv = wlv_ref[rows, hs]
            acc = jnp.zeros((S_BLK, HEAD_DIM), F32)
            for b8 in range(8):
                c_old = c_ref[0, sb * 8 + b8, h]
                sel = sub == b8
                acc = jnp.where(sel, _mm(q, c_old.astype(BF16)), acc)
                upd = lax.dot_general(k, jnp.where(sel, wlv, jnp.zeros_like(wlv)), (((0,), (0,)), ((), ())),
                                      preferred_element_type=F32)
                dl = colt_ref[sb * S_BLK + b8:sb * S_BLK + b8 + 1, COL_DL + h:COL_DL + h + 1]
                cnew_ref[0, sb * 8 + b8, h] = dl * c_old + upd
            qc_ref[rows, hs] = acc


S_SB = 2


def _sample_state_call(c_state, q_bf, k_bf, wlv_bf, colt):
    row_spec = lambda w: pl.BlockSpec((S_SB * S_BLK, w), lambda i: (i, 0))
    c_spec = pl.BlockSpec((1, S_SB * 8, N_HEADS, HEAD_DIM, HEAD_DIM), lambda i: (0, i, 0, 0, 0))
    return pl.pallas_call(
        _sample_state_body,
        grid=(S_RB // S_SB,),
        in_specs=[c_spec, row_spec(MIX), row_spec(MIX), row_spec(MIX), row_spec(128)],
        out_specs=(row_spec(MIX), c_spec),
        out_shape=(jax.ShapeDtypeStruct((ROWS, MIX), F32), jax.ShapeDtypeStruct(c_state.shape, F32)),
        compiler_params=pltpu.CompilerParams(dimension_semantics=("arbitrary",), vmem_limit_bytes=VMEM_LIMIT),
        name="sample_state",
    )(c_state, q_bf, k_bf, wlv_bf, colt)


def _sample_back_body(x_ref, p_ref, num_ref, dec_ref, den_ref, qc_ref, og_ref, zg_ref, xc_ref, yb_ref,
                      wout_ref, wple_ref, wpg_ref, lnhead_ref, skip_ref, lnple_ref, lnfin_ref, y_ref,
                      xtm_s, ptm_s):
    ha = (num_ref[...] + dec_ref[...] * qc_ref[...]) / den_ref[...] * og_ref[...]
    ha = jnp.concatenate([_head_norm(ha[:, h * HEAD_DIM:(h + 1) * HEAD_DIM]) for h in range(N_HEADS)], axis=-1)
    ha = (ha * lnhead_ref[...] + skip_ref[...] * xc_ref[...]) * zg_ref[...]
    e_raw = _mm(_rows_to_blocks(p_ref, ptm_s).astype(BF16), wple_ref[...])
    y = _merge_out(_rows_to_blocks(x_ref, xtm_s), e_raw, ha, yb_ref[...], wout_ref, wpg_ref, lnple_ref, lnfin_ref)
    for rb in range(S_RB):
        for t in range(S_T):
            y_ref[rb * 8:(rb + 1) * 8, t, :] = y[rb * S_BLK + t * 8:rb * S_BLK + t * 8 + 8, :]


def _sample_back_call(x3, p3, num, dec, den, qc, og, zg, xc, yb, weights, vecs):
    (_, _, _, _, _, _, _, _, wout, wple, wpg) = weights
    (_, _, _, _, _, lnhead, skip, _, _, lnple, lnfin) = vecs
    return pl.pallas_call(
        _sample_back_body, out_shape=jax.ShapeDtypeStruct(x3.shape, F32),
        scratch_shapes=[pltpu.VMEM((ROWS, D_MODEL), F32), pltpu.VMEM((ROWS, D_PLE), F32)],
        compiler_params=pltpu.CompilerParams(vmem_limit_bytes=VMEM_LIMIT),
        name="sample_back",
    )(x3, p3, num, dec, den, qc, og, zg, xc, yb, wout, wple, wpg, lnhead, skip, lnple, lnfin)


def _sample_path(x_sample, p_sample, c_state, n_state, m_state, conv_state, sre_state, sim_state,
                 weights, vecs, are8, aim8):
    nb, n_t, _ = x_sample.shape
    assert nb == S_RB * 8 and n_t == S_T
    m_rows = jnp.broadcast_to(jnp.transpose(m_state).reshape(N_HEADS, S_RB, 1, 8), (N_HEADS, S_RB, S_T, 8))
    mrow_in = jnp.pad(m_rows.reshape(N_HEADS, ROWS), ((0, 8 - N_HEADS), (0, 0)))
    s5_t = lambda s: jnp.transpose(s, (1, 2, 0)).reshape(N_STATE, nb)
    (q_bf, k_bf, wlv_bf, colt, num, dec, den, og, zg, xc, yb, n_new, mrow, conv_new, sre, sim) = _sample_front_call(
        x_sample, weights, vecs, are8, aim8, n_state, mrow_in, jnp.transpose(conv_state, (1, 0, 2)),
        s5_t(sre_state), s5_t(sim_state))
    qc, c_new = _sample_state_call(c_state, q_bf, k_bf, wlv_bf, colt)
    y_s = _sample_back_call(x_sample, p_sample, num, dec, den, qc, og, zg, xc, yb, weights, vecs)
    m_new = jnp.transpose(mrow[0:N_HEADS].reshape(N_HEADS, S_RB, S_T, 8)[:, :, 0, :].reshape(N_HEADS, nb))
    s5_back = lambda s: jnp.transpose(s.reshape(N_GROUPS, S5_STATE, nb), (2, 0, 1))[None]
    return (y_s, c_new, n_new[None], m_new[None], jnp.transpose(conv_new, (1, 0, 2))[None],
            s5_back(sre), s5_back(sim))


IN_SPLITS = (0, MIX, 2 * MIX, 3 * MIX + 2 * N_HEADS, 4 * MIX + 2 * N_HEADS)
GATE_ROW = 3 * MIX


def _weight_prep_body(wint_ref, wq_ref, wk_ref, wv_ref, wglu_ref, wout_ref, wple_ref, wpg_ref,
                      wbig_ref, wgate_ref, wqk_ref, wvo_ref, wgluo_ref, wouto_ref, wpleo_ref, wpgo_ref):
    for s, r0 in enumerate(IN_SPLITS):
        wbig_ref[:, s * MIX:(s + 1) * MIX] = wint_ref[r0:r0 + MIX, :].T.astype(BF16)
    g8 = wint_ref[GATE_ROW:GATE_ROW + 8, :]
    row = lax.broadcasted_iota(jnp.int32, g8.shape, 0)
    zero = jnp.zeros_like(g8)
    top = jnp.where(row < N_HEADS, g8, zero)
    bot = jnp.where(row < N_HEADS, pltpu.roll(g8, N_HEADS, axis=0), zero)
    wgate_ref[...] = jnp.concatenate([top, bot], axis=0).astype(BF16)
    for h in range(N_HEADS):
        wqk_ref[h, :, 0:HEAD_DIM] = wq_ref[0, h].astype(BF16)
        wqk_ref[h, :, HEAD_DIM:2 * HEAD_DIM] = wk_ref[0, h].astype(BF16)
    wvo_ref[...] = wv_ref[0].astype(BF16)
    wgluo_ref[...] = wglu_ref[0].astype(BF16)
    wouto_ref[...] = wout_ref[0].astype(BF16)
    wpleo_ref[...] = wple_ref[0].astype(BF16)
    wpgo_ref[...] = wpg_ref[0].astype(BF16)


def _weight_prep_call(w_in_t, w_q, w_k, w_v, w_glu, w_out, w_ple, w_pg):
    bf = lambda *s: jax.ShapeDtypeStruct(s, BF16)
    return pl.pallas_call(
        _weight_prep_body,
        out_shape=(bf(D_MODEL, 5 * MIX), bf(16, D_MODEL), bf(N_HEADS, HEAD_DIM, 2 * HEAD_DIM),
                   bf(N_HEADS, HEAD_DIM, HEAD_DIM), bf(MIX, MIX), bf(2 * MIX, D_MODEL), bf(D_PLE, D_MODEL),
                   bf(D_MODEL, D_MODEL)),
        compiler_params=pltpu.CompilerParams(vmem_limit_bytes=VMEM_LIMIT),
        name="weight_prep",
    )(w_in_t, w_q, w_k, w_v, w_glu, w_out, w_ple, w_pg)


def _layout_params(ln_mix, w_in, b_igate, b_fgate, conv_w, conv_b, w_q, w_k, w_v, ln_head, skip_a,
                   s5_lam_re, s5_lam_im, s5_log_dt, s5_B_re, s5_B_im, s5_C_re, s5_C_im, s5_D, w_glu, b_glu,
                   w_out, w_ple, ln_ple, w_ple_gate, ln_final):
    wbig, wgate, wqk, wv, wglu_bf, wout_bf, wple_bf, wpg_bf = _weight_prep_call(
        jnp.transpose(w_in[0]), w_q, w_k, w_v, w_glu, w_out, w_ple, w_ple_gate)
    b_gcp = lambda m: jnp.transpose(m[0], (0, 2, 1)).reshape(2, 16 * S5_GROUP, S5_STATE)
    c_gcp = lambda m: m[0].reshape(2, 16 * S5_GROUP, S5_STATE)
    lam = jnp.concatenate([v.reshape(1, N_STATE) for v in (s5_lam_re, s5_lam_im, s5_log_dt)], axis=0)
    are8, aim8, wb, wcre, wcim = _prep_call(lam, b_gcp(s5_B_re), b_gcp(s5_B_im), c_gcp(s5_C_re), c_gcp(s5_C_im))
    weights = [wbig, wgate, wqk, wv, wb, wcre, wcim, wglu_bf, wout_bf, wple_bf, wpg_bf]
    pad8 = lambda v: jnp.pad(v[0][:, None], ((0, 8 - N_HEADS), (0, 0)))
    vecs = [ln_mix[0][None, :], conv_w[0], conv_b[0][None, :], pad8(b_igate), pad8(b_fgate),
            ln_head[0][None, :], skip_a[0][None, :], s5_D[0][None, :], b_glu[0][None, :],
            ln_ple[0][None, :], ln_final[None, :]]
    return weights, vecs, are8, aim8


def kernel(x_prompt, x_sample, p_prompt, p_sample, state_mlstm_C, state_mlstm_n, state_mlstm_m, state_conv, state_s5_re, state_s5_im, ln_mix, w_in, b_igate, b_fgate, conv_w, conv_b, w_q, w_k, w_v, ln_head, skip_a, s5_lam_re, s5_lam_im, s5_log_dt, s5_B_re, s5_B_im, s5_C_re, s5_C_im, s5_D, w_glu, b_glu, w_out, w_ple, ln_ple, w_ple_gate, ln_final):
    weights, vecs, are8, aim8 = _layout_params(
        ln_mix, w_in, b_igate, b_fgate, conv_w, conv_b, w_q, w_k, w_v, ln_head, skip_a, s5_lam_re, s5_lam_im,
        s5_log_dt, s5_B_re, s5_B_im, s5_C_re, s5_C_im, s5_D, w_glu, b_glu, w_out, w_ple, ln_ple, w_ple_gate,
        ln_final)

    nbp = x_prompt.shape[0]
    y_p, c_p, n_p, mrow_p, conv_t, sre_p, sim_p = _prompt_call(x_prompt, p_prompt[0], weights, vecs, are8, aim8)
    conv_p = jnp.transpose(conv_t, (1, 0, 2))[None]
    pn = n_p.reshape(1, nbp, N_HEADS, HEAD_DIM)
    pm = jnp.transpose(mrow_p[0:N_HEADS, ::P_T])[None]
    pre = sre_p.reshape(1, nbp, N_GROUPS, S5_STATE)
    pim = sim_p.reshape(1, nbp, N_GROUPS, S5_STATE)

    y_s, c_s, n_s, m_s, conv_s, sre_s, sim_s = _sample_path(
        x_sample, p_sample[0], state_mlstm_C, state_mlstm_n[0], state_mlstm_m[0], state_conv[0],
        state_s5_re[0], state_s5_im[0], weights, vecs, are8, aim8)
    return (y_p, y_s, c_p, pn, pm, conv_p, pre, pim, c_s, n_s, m_s, conv_s, sre_s, sim_s)
```

```python
import functools
import math

import jax
import jax.numpy as jnp
from jax import lax
from jax.experimental import pallas as pl
from jax.experimental.pallas import tpu as pltpu

F32 = jnp.float32
BF16 = jnp.bfloat16

D_MODEL = 1024
D_PLE = 256
MIX = 512
N_HEADS = 4
HEAD_DIM = 128
N_GROUPS = 32
S5_GROUP = 16
S5_STATE = 64
N_STATE = N_GROUPS * S5_STATE
CONV_W = 4
EPS = 1e-6
K_SCALE = HEAD_DIM ** -0.5
ROWS = 512
GROUP_ROWS = 256
VMEM_LIMIT = 60 * 1024 * 1024


def _mm(a, w):
    return jnp.dot(a, w, preferred_element_type=F32)


def _rms(x, g):
    ms = jnp.mean(x * x, axis=-1, keepdims=True)
    return x * lax.rsqrt(ms + EPS) * g


def _sigmoid(x):
    return 0.5 * jnp.tanh(0.5 * x) + 0.5


def _silu(x):
    h = 0.5 * x
    return h + h * jnp.tanh(h)


def _gelu_tanh(x):
    c = math.sqrt(2.0 / math.pi)
    return 0.5 * x * (1.0 + jnp.tanh(c * (x + 0.044715 * (x * x * x))))


def _log_sigmoid(x):
    return jnp.minimum(x, 0.0) - jnp.log1p(jnp.exp(-jnp.abs(x)))


SCAN_RADIX = 8


def _seg_scan(x, tidx, n_t, t_stride, op, fill):
    span = 1
    while span < n_t:
        acc = x
        for j in range(1, SCAN_RADIX):
            if j * span >= n_t:
                break
            prev = pltpu.roll(x, j * span * t_stride, axis=1)
            acc = op(acc, jnp.where(tidx >= j * span, prev, fill))
        x = acc
        span *= SCAN_RADIX
    return x


def _bcast_last(x, tidx, n_t, t_stride):
    y = jnp.where(tidx == n_t - 1, x, 0.0)
    span = 1
    while span < n_t:
        acc = y
        for j in range(1, SCAN_RADIX):
            if j * span >= n_t:
                break
            nxt = pltpu.roll(y, ROWS - j * span * t_stride, axis=1)
            acc = acc + jnp.where(tidx + j * span <= n_t - 1, nxt, 0.0)
        y = acc
        span *= SCAN_RADIX
    return y


def _gate_rows(a, wg_ref, bi8, bf8, mprev, n_t, t_stride):
    gt = lax.dot_general(wg_ref[...], a, (((1,), (1,)), ((), ())), preferred_element_type=F32)
    lane = lax.broadcasted_iota(jnp.int32, (8, ROWS), 1)
    tidx = (lane // t_stride) % n_t
    i8 = gt[0:8] + bi8
    logf = _log_sigmoid(gt[8:16] + bf8)
    b = _seg_scan(logf, tidx, n_t, t_stride, jnp.add, 0.0)
    c = i8 - b
    cm = _seg_scan(c, tidx, n_t, t_stride, jnp.maximum, -jnp.inf)
    big_m = jnp.maximum(mprev, cm)
    mt = b + big_m
    m_last = _bcast_last(big_m, tidx, n_t, t_stride)
    mnew = _bcast_last(mt, tidx, n_t, t_stride)
    stack = jnp.concatenate(
        [-big_m, jnp.exp(mprev - big_m), jnp.exp(-mt), jnp.exp(c - m_last), jnp.exp(mprev - m_last),
         jnp.zeros((128 - 40, ROWS), F32)], axis=0)
    return c, stack.T, mnew


COL_NEGM, COL_DECAY, COL_EM, COL_WL, COL_DL = 0, 8, 16, 24, 32


def _intra(qg, kg, vg, col_a, row_b, mask):
    s = lax.dot_general(qg, kg, (((1,), (1,)), ((), ())), preferred_element_type=F32)
    p = s * jnp.exp(jnp.where(mask, col_a + row_b, -jnp.inf))
    rs = jnp.sum(p, axis=-1, keepdims=True)
    return _mm(p.astype(BF16), vg), rs


def _head_norm(x):
    mu = jnp.mean(x, axis=-1, keepdims=True)
    xc = x - mu
    var = jnp.mean(xc * xc, axis=-1, keepdims=True)
    return xc * lax.rsqrt(var + EPS)


S5_HALF = N_STATE // 2


S5_CHUNK = 512
S5_N_CHUNKS = N_STATE // S5_CHUNK


def _s5_scan(views, are_ref, aim_ref, sre_ref, sim_ref, row_base, st_rows, n_t, chunks=range(S5_N_CHUNKS)):
    chunk = S5_CHUNK
    assert n_t % 2 == 0 and row_base % 16 == 0
    for gch in chunks:
        hf, ch = divmod(gch, S5_HALF // chunk)
        bure_s, buim_s, xsre_s, xsim_s = views[hf]
        ls = slice(ch * chunk, (ch + 1) * chunk)
        gl = slice(hf * S5_HALF + ch * chunk, hf * S5_HALF + (ch + 1) * chunk)
        ar = are_ref[:, gl]
        ai = aim_ref[:, gl]
        xr = sre_ref[st_rows, gl]
        xi = sim_ref[st_rows, gl]
        for t in range(0, n_t, 2):
            r0 = row_base + t * 8
            xr1, xi1 = (ar * xr - ai * xi + bure_s[r0:r0 + 8, ls], ar * xi + ai * xr + buim_s[r0:r0 + 8, ls])
            xr, xi = (ar * xr1 - ai * xi1 + bure_s[r0 + 8:r0 + 16, ls],
                      ar * xi1 + ai * xr1 + buim_s[r0 + 8:r0 + 16, ls])
            xsre_s[r0:r0 + 16, ls] = jnp.concatenate([xr1, xr], axis=0).astype(BF16)
            xsim_s[r0:r0 + 16, ls] = jnp.concatenate([xi1, xi], axis=0).astype(BF16)
        sre_ref[st_rows, gl] = xr
        sim_ref[st_rows, gl] = xi


def _s5_views(bure_s, buim_s, xsre_s, xsim_s):
    return [[r.at[hf] for r in (bure_s, buim_s, xsre_s, xsim_s)] for hf in range(2)]


def _s5_in(views, u_tm, wb_ref):
    for hf in range(2):
        res = _mm(u_tm[:, hf * 256:(hf + 1) * 256].astype(BF16), wb_ref[hf])
        views[hf][0][...] = res[:, :S5_HALF]
        views[hf][1][...] = res[:, S5_HALF:]


def _s5_out(views, wcre_ref, wcim_ref):
    return [_mm(views[hf][2][...], wcre_ref[hf]) - _mm(views[hf][3][...], wcim_ref[hf]) for hf in range(2)]


def _s5_post(y_s5, u, zs, wglu_ref, s5d_ref, bglu_ref):
    yb = _gelu_tanh(y_s5 + s5d_ref[...] * u)
    yb = yb * _sigmoid(_mm(yb.astype(BF16), wglu_ref[...]) + bglu_ref[...])
    return yb * _silu(zs)


OUT_BLOCK = 512


def _merge_out(x, e_raw, h_a, y_b, wout_ref, wpg_ref, lnple_ref, lnfin_ref):
    h1 = x + _mm(h_a.astype(BF16), wout_ref[0:MIX, :]) + _mm(y_b.astype(BF16), wout_ref[MIX:2 * MIX, :])
    e_half = 0.5 * _rms(e_raw, lnple_ref[...])
    h1_bf = h1.astype(BF16)
    base = h1 + e_half
    blocks = []
    ssq = jnp.zeros((h1.shape[0], 1), F32)
    for nb in range(D_MODEL // OUT_BLOCK):
        cs = slice(nb * OUT_BLOCK, (nb + 1) * OUT_BLOCK)
        h2 = base[:, cs] + e_half[:, cs] * jnp.tanh(0.5 * _mm(h1_bf, wpg_ref[:, cs]))
        ssq = ssq + jnp.sum(h2 * h2, axis=-1, keepdims=True)
        blocks.append(h2)
    scale = lax.rsqrt(ssq * (1.0 / D_MODEL) + EPS)
    return jnp.concatenate(blocks, axis=-1) * scale * lnfin_ref[...]


def _prep_body(lam_ref, btr_ref, bti_ref, ctr_ref, cti_ref, are_ref, aim_ref, wb_ref, wcre_ref, wcim_ref):
    lr = jnp.minimum(lam_ref[0:1, :], -1e-4)
    li = lam_ref[1:2, :]
    dt = jnp.exp(lam_ref[2:3, :])
    mag = jnp.exp(lr * dt)
    a_re = mag * jnp.cos(li * dt)
    a_im = mag * jnp.sin(li * dt)
    den = lr * lr + li * li
    xr = a_re - 1.0
    g_re = (xr * lr + a_im * li) / den
    g_im = (a_im * lr - xr * li) / den
    are_ref[...] = jnp.broadcast_to(a_re, (8, N_STATE))
    aim_ref[...] = jnp.broadcast_to(a_im, (8, N_STATE))
    half = N_STATE // 2
    own_block = ((lax.broadcasted_iota(jnp.int32, (256, half), 0) // S5_GROUP)
                 == (lax.broadcasted_iota(jnp.int32, (256, half), 1) // S5_STATE))
    blocks = lambda ref, hf: jnp.where(own_block, jnp.concatenate([ref[hf]] * 16, axis=1), 0.0)
    for hf in range(2):
        gr = g_re[:, hf * half:(hf + 1) * half]
        gi = g_im[:, hf * half:(hf + 1) * half]
        br = blocks(btr_ref, hf)
        bi = blocks(bti_ref, hf)
        wb_ref[hf, :, 0:half] = (gr * br - gi * bi).astype(BF16)
        wb_ref[hf, :, half:2 * half] = (gr * bi + gi * br).astype(BF16)
        wcre_ref[hf] = blocks(ctr_ref, hf).T.astype(BF16)
        wcim_ref[hf] = blocks(cti_ref, hf).T.astype(BF16)


def _prep_call(lam, bt_re, bt_im, ct_re, ct_im):
    half = N_STATE // 2
    return pl.pallas_call(
        _prep_body,
        out_shape=(jax.ShapeDtypeStruct((8, N_STATE), F32), jax.ShapeDtypeStruct((8, N_STATE), F32),
                   jax.ShapeDtypeStruct((2, 256, N_STATE), BF16),
                   jax.ShapeDtypeStruct((2, half, 256), BF16), jax.ShapeDtypeStruct((2, half, 256), BF16)),
        compiler_params=pltpu.CompilerParams(vmem_limit_bytes=VMEM_LIMIT),
        name="s5_prep",
    )(lam, bt_re, bt_im, ct_re, ct_im)


P_T = 64
P_NB = 8


P_SUB = 2


def _prompt_body(*refs):
    n_in = 26
    c_ref, n_ref, mrow_ref = refs[n_in + 1:n_in + 4]
    sre_ref, sim_ref, ext_s = refs[n_in + 5:n_in + 8]
    assert len(refs) == n_in + 7 + 9

    @pl.when(pl.program_id(0) == 0)
    def _init():
        c_ref[...] = jnp.zeros_like(c_ref)
        n_ref[...] = jnp.zeros_like(n_ref)
        mrow_ref[...] = jnp.zeros_like(mrow_ref)
        sre_ref[...] = jnp.zeros_like(sre_ref)
        sim_ref[...] = jnp.zeros_like(sim_ref)
        ext_s[:, 0:8, :] = jnp.zeros((P_NB, 8, MIX), F32)

    def one_tile(sub, carry):
        _prompt_tile(pl.ds(pl.multiple_of(sub * P_T, P_T), P_T), *refs)
        return carry

    lax.fori_loop(0, P_SUB, one_tile, 0)


def _prompt_tile(ts, x_ref, p_ref, wbig_ref, wgate_ref, wqk_ref, wv_ref, wb_ref, wcre_ref, wcim_ref,
                 wglu_ref, wout_ref, wple_ref, wpg_ref, lnmix_ref, convw_ref, convb_ref, bi8_ref, bf8_ref,
                 lnhead_ref, skip_ref, s5d_ref, bglu_ref, lnple_ref, lnfin_ref, are_ref, aim_ref,
                 y_ref, c_ref, n_ref, mrow_ref, conv_ref, sre_ref, sim_ref,
                 ext_s, ha_s, utm_s, ytm_s, ybm_s, bure_s, buim_s, xsre_s, xsim_s):
    x = x_ref[:, ts, :].reshape(ROWS, D_MODEL)
    a = _rms(x, lnmix_ref[...]).astype(BF16)

    pair_w = 2 * HEAD_DIM
    xc_parts, xc_bf_parts, xm_bf_parts = [], [], []
    xms = []
    for pr in range(N_HEADS // 2):
        ps = slice(pr * pair_w, (pr + 1) * pair_w)
        xm = _mm(a, wbig_ref[:, ps])
        ext_s[:, 8:8 + P_T, ps] = xm.reshape(P_NB, P_T, pair_w)
        xms.append(xm)

    c8, colt, mnew = _gate_rows(a, wgate_ref, bi8_ref[...], bf8_ref[...], mrow_ref[...], P_T, 1)
    mrow_ref[...] = mnew

    for pr in range(N_HEADS // 2):
        ps = slice(pr * pair_w, (pr + 1) * pair_w)
        xm = xms[pr]
        yc = convb_ref[:, ps]
        for j in range(CONV_W):
            yc = yc + ext_s[:, 5 + j:5 + j + P_T, ps].reshape(ROWS, pair_w) * convw_ref[j:j + 1, ps]
        for j in range(CONV_W - 1):
            conv_ref[j, :, ps] = ext_s[:, 5 + P_T + j, ps]
        ext_s[:, 5:8, ps] = ext_s[:, 5 + P_T:8 + P_T, ps]
        xc_pr = _silu(yc)
        xc_parts.append(xc_pr)
        xc_bf_parts.append(xc_pr.astype(BF16))
        xm_bf_parts.append(xm.astype(BF16))
    xc = jnp.concatenate(xc_parts, axis=-1)

    zs_raw = _mm(a, wbig_ref[:, 2048:2560])
    e_raw = _mm(p_ref[:, ts, :].reshape(ROWS, D_PLE).astype(BF16), wple_ref[...])

    views = _s5_views(bure_s, buim_s, xsre_s, xsim_s)

    def s5_front():
        u = _mm(a, wbig_ref[:, 1536:2048])
        for b in range(P_NB):
            for j in range(P_T // 8):
                for lc in range(MIX // 128):
                    utm_s[lc, pl.ds(j * 64 + b, 8, stride=8), :] = (
                        u[b * P_T + j * 8:b * P_T + j * 8 + 8, lc * 128:(lc + 1) * 128])
        _s5_in(views, jnp.concatenate([utm_s[lc] for lc in range(MIX // 128)], axis=-1), wb_ref)
        _s5_scan(views, are_ref, aim_ref, sre_ref, sim_ref, 0, slice(0, 8), P_T)
        return u

    def s5_back(u):
        y_halves = _s5_out(views, wcre_ref, wcim_ref)
        for hf, y_half in enumerate(y_halves):
            ytm_s[2 * hf] = y_half[:, 0:128]
            ytm_s[2 * hf + 1] = y_half[:, 128:256]
        for b in range(P_NB):
            for j in range(P_T // 8):
                for lc in range(MIX // 128):
                    ybm_s[b * P_T + j * 8:b * P_T + j * 8 + 8, lc * 128:(lc + 1) * 128] = (
                        ytm_s[lc, pl.ds(j * 64 + b, 8, stride=8), :])
        return _s5_post(ybm_s[...], u, zs_raw, wglu_ref, s5d_ref, bglu_ref)

    r = lax.broadcasted_iota(jnp.int32, (GROUP_ROWS, GROUP_ROWS), 0)
    cc = lax.broadcasted_iota(jnp.int32, (GROUP_ROWS, GROUP_ROWS), 1)
    mask = ((r // P_T) == (cc // P_T)) & (cc <= r)

    for h in range(N_HEADS):
        hs = slice(h * HEAD_DIM, (h + 1) * HEAD_DIM)
        hp = slice((h % 2) * HEAD_DIM, (h % 2 + 1) * HEAD_DIM)
        qk = _mm(xc_bf_parts[h // 2][:, hp], wqk_ref[h])
        q_f = qk[:, :HEAD_DIM]
        k_f = qk[:, HEAD_DIM:] * K_SCALE
        v_f = _mm(xm_bf_parts[h // 2][:, hp], wv_ref[h])
        q_bf = q_f.astype(BF16)
        k_bf = k_f.astype(BF16)
        v_bf = v_f.astype(BF16)
        for g in range(ROWS // GROUP_ROWS):
            gs = slice(g * GROUP_ROWS, (g + 1) * GROUP_ROWS)
            decay = colt[gs, COL_DECAY + h:COL_DECAY + h + 1]
            wl = colt[gs, COL_WL + h:COL_WL + h + 1]
            dl = colt[gs, COL_DL + h:COL_DL + h + 1]
            num, rs = _intra(q_bf[gs], k_bf[gs], v_bf[gs], colt[gs, COL_NEGM + h:COL_NEGM + h + 1],
                             c8[h:h + 1, gs], mask)
            wlv = (wl * v_f[gs]).astype(BF16)
            wlk = wl * k_f[gs]
            qcs, qns = [], []
            for bb in range(GROUP_ROWS // P_T):
                b = g * (GROUP_ROWS // P_T) + bb
                rb = slice(g * GROUP_ROWS + bb * P_T, g * GROUP_ROWS + (bb + 1) * P_T)
                rl = slice(bb * P_T, (bb + 1) * P_T)
                c_old = c_ref[0, b, h]
                n_old = n_ref[b:b + 1, hs]
                qcs.append(_mm(q_bf[rb], c_old.astype(BF16)))
                qns.append(jnp.sum(q_f[rb] * n_old, axis=-1, keepdims=True))
                upd = lax.dot_general(k_bf[rb], wlv[rl], (((0,), (0,)), ((), ())), preferred_element_type=F32)
                dlb = dl[rl]
                c_ref[0, b, h] = jnp.concatenate([dlb, dlb], axis=0) * c_old + upd
                n_ref[b:b + 1, hs] = dlb[0:1] * n_old + jnp.sum(wlk[rl], axis=0, keepdims=True)
            num = num + decay * jnp.concatenate(qcs, axis=0)
            qn = rs + decay * jnp.concatenate(qns, axis=0)
            den = jnp.maximum(jnp.abs(qn), colt[gs, COL_EM + h:COL_EM + h + 1])
            ha_s[gs, hs] = num / den

    yb = s5_back(s5_front())
    ha = ha_s[...] * _sigmoid(_mm(a, wbig_ref[:, 1024:1536]))
    ha = jnp.concatenate([_head_norm(ha[:, h * HEAD_DIM:(h + 1) * HEAD_DIM]) for h in range(N_HEADS)], axis=-1)
    ha = (ha * lnhead_ref[...] + skip_ref[...] * xc) * _silu(_mm(a, wbig_ref[:, 512:1024]))

    y = _merge_out(x, e_raw, ha, yb, wout_ref, wpg_ref, lnple_ref, lnfin_ref)
    y_ref[:, ts, :] = y.reshape(P_NB, P_T, D_MODEL)


def _const_spec(shape):
    nd = len(shape)
    return pl.BlockSpec(shape, lambda i, nd=nd: (0,) * nd, pipeline_mode=pl.Buffered(1))


def _prompt_call(x, p, weights, vecs, are8, aim8):
    nb, seq, _ = x.shape
    blk_t = P_T * P_SUB
    assert nb == P_NB and seq % blk_t == 0
    n_steps = seq // blk_t
    w_specs = [_const_spec(w.shape) for w in weights]
    v_specs = [_const_spec(v.shape) for v in vecs]
    in_specs = ([pl.BlockSpec((P_NB, blk_t, D_MODEL), lambda i: (0, i, 0)),
                 pl.BlockSpec((P_NB, blk_t, D_PLE), lambda i: (0, i, 0))]
                + w_specs + v_specs + [_const_spec((8, N_STATE)), _const_spec((8, N_STATE))])
    out_shape = (
        jax.ShapeDtypeStruct((nb, seq, D_MODEL), F32),
        jax.ShapeDtypeStruct((1, nb, N_HEADS, HEAD_DIM, HEAD_DIM), F32),
        jax.ShapeDtypeStruct((nb, MIX), F32),
        jax.ShapeDtypeStruct((8, ROWS), F32),
        jax.ShapeDtypeStruct((CONV_W - 1, nb, MIX), F32),
        jax.ShapeDtypeStruct((8, N_STATE), F32),
        jax.ShapeDtypeStruct((8, N_STATE), F32),
    )
    out_specs = (
        pl.BlockSpec((P_NB, blk_t, D_MODEL), lambda i: (0, i, 0)),
        pl.BlockSpec((1, nb, N_HEADS, HEAD_DIM, HEAD_DIM), lambda i: (0, 0, 0, 0, 0)),
        pl.BlockSpec((nb, MIX), lambda i: (0, 0)),
        pl.BlockSpec((8, ROWS), lambda i: (0, 0)),
        pl.BlockSpec((CONV_W - 1, nb, MIX), lambda i: (0, 0, 0)),
        pl.BlockSpec((8, N_STATE), lambda i: (0, 0)),
        pl.BlockSpec((8, N_STATE), lambda i: (0, 0)),
    )
    scratch = [
        pltpu.VMEM((P_NB, P_T + 8, MIX), F32),
        pltpu.VMEM((ROWS, MIX), F32),
        pltpu.VMEM((MIX // 128, ROWS, 128), F32),
        pltpu.VMEM((MIX // 128, ROWS, 128), F32),
        pltpu.VMEM((ROWS, MIX), F32),
        pltpu.VMEM((2, ROWS, S5_HALF), F32),
        pltpu.VMEM((2, ROWS, S5_HALF), F32),
        pltpu.VMEM((2, ROWS, S5_HALF), BF16),
        pltpu.VMEM((2, ROWS, S5_HALF), BF16),
    ]
    return pl.pallas_call(
        _prompt_body,
        grid=(n_steps,),
        in_specs=in_specs,
        out_specs=out_specs,
        out_shape=out_shape,
        scratch_shapes=scratch,
        compiler_params=pltpu.CompilerParams(dimension_semantics=("arbitrary",), vmem_limit_bytes=VMEM_LIMIT),
        name="prompt_layer",
    )(x, p, *weights, *vecs, are8, aim8)


S_T = 4
S_RB = ROWS // (8 * S_T)
S_BLK = 8 * S_T


def _sample_front_body(x_ref, wbig_ref, wgate_ref, wqk_ref, wv_ref, wb_ref, wcre_ref, wcim_ref, wglu_ref,
                       lnmix_ref, convw_ref, convb_ref, bi8_ref, bf8_ref, s5d_ref, bglu_ref, are_ref, aim_ref,
                       n_in_ref, mrow_in_ref, conv_in_ref, sre_in_ref, sim_in_ref,
                       q_ref, k_ref, wlv_ref, colt_ref, num_ref, dec_ref, den_ref, og_ref, zg_ref, xc_ref, yb_ref,
                       n_ref, mrow_ref, conv_ref, sre_out_ref, sim_out_ref,
                       ext_s, bure_s, buim_s, xsre_s, xsim_s, xtm_s, sre_ref, sim_ref):
    x = _rows_to_blocks(x_ref, xtm_s)
    a = _rms(x, lnmix_ref[...]).astype(BF16)

    xm = _mm(a, wbig_ref[:, 0:512])
    ext_rows = (CONV_W - 1 + S_T) * 8
    ycs = []
    for rb in range(S_RB):
        seqs = slice(rb * 8, (rb + 1) * 8)
        for j in range(CONV_W - 1):
            ext_s[rb * ext_rows + 8 * j:rb * ext_rows + 8 * j + 8, :] = conv_in_ref[j, seqs, :]
        ext_s[rb * ext_rows + 24:(rb + 1) * ext_rows, :] = xm[rb * S_BLK:(rb + 1) * S_BLK, :]
        yc = convb_ref[...]
        for j in range(CONV_W):
            yc = yc + ext_s[rb * ext_rows + 8 * j:rb * ext_rows + 8 * j + S_BLK, :] * convw_ref[j:j + 1, :]
        ycs.append(yc)
        for j in range(CONV_W - 1):
            conv_ref[j, seqs, :] = ext_s[rb * ext_rows + S_BLK + 8 * j:rb * ext_rows + S_BLK + 8 * j + 8, :]
    xc = _silu(jnp.concatenate(ycs, axis=0))
    xc_ref[...] = xc
    xc_bf = xc.astype(BF16)
    xm_bf = xm.astype(BF16)

    mprev = mrow_in_ref[...]
    c8, colt, mnew = _gate_rows(a, wgate_ref, bi8_ref[...], bf8_ref[...], mprev, S_T, 8)
    mrow_ref[...] = mnew
    colt_ref[...] = colt

    r = lax.broadcasted_iota(jnp.int32, (GROUP_ROWS, GROUP_ROWS), 0)
    cc = lax.broadcasted_iota(jnp.int32, (GROUP_ROWS, GROUP_ROWS), 1)
    same_seq = ((r // S_BLK) == (cc // S_BLK)) & ((r % 8) == (cc % 8))
    mask = same_seq & (((cc // 8) % S_T) <= ((r // 8) % S_T))

    nb = S_RB * 8
    for h in range(N_HEADS):
        hs = slice(h * HEAD_DIM, (h + 1) * HEAD_DIM)
        qk = _mm(xc_bf[:, hs], wqk_ref[h])
        q_f = qk[:, :HEAD_DIM]
        k_f = qk[:, HEAD_DIM:] * K_SCALE
        v_f = _mm(xm_bf[:, hs], wv_ref[h])
        q_bf = q_f.astype(BF16)
        k_bf = k_f.astype(BF16)
        v_bf = v_f.astype(BF16)
        decay = colt[:, COL_DECAY + h:COL_DECAY + h + 1]
        wl = colt[:, COL_WL + h:COL_WL + h + 1]
        dl = colt[:, COL_DL + h:COL_DL + h + 1]
        nums, rss = [], []
        for g in range(ROWS // GROUP_ROWS):
            gs = slice(g * GROUP_ROWS, (g + 1) * GROUP_ROWS)
            num, rs = _intra(q_bf[gs], k_bf[gs], v_bf[gs], colt[gs, COL_NEGM + h:COL_NEGM + h + 1],
                             c8[h:h + 1, gs], mask)
            nums.append(num)
            rss.append(rs)
        q_ref[:, hs] = q_bf
        k_ref[:, hs] = k_bf
        wlv_ref[:, hs] = (wl * v_f).astype(BF16)
        num_ref[:, hs] = jnp.concatenate(nums, axis=0)
        n_old = n_in_ref[:, h, :]
        n_rows = jnp.broadcast_to(n_old.reshape(S_RB, 1, 8, HEAD_DIM), (S_RB, S_T, 8, HEAD_DIM)).reshape(ROWS, HEAD_DIM)
        qn = jnp.concatenate(rss, axis=0) + decay * jnp.sum(q_f * n_rows, axis=-1, keepdims=True)
        den = jnp.maximum(jnp.abs(qn), colt[:, COL_EM + h:COL_EM + h + 1])
        dec_ref[:, hs] = jnp.broadcast_to(decay, (ROWS, HEAD_DIM))
        den_ref[:, hs] = jnp.broadcast_to(den, (ROWS, HEAD_DIM))
        wlk_sum = jnp.sum((wl * k_f).reshape(S_RB, S_T, 8, HEAD_DIM), axis=1).reshape(nb, HEAD_DIM)
        dl_seq = dl.reshape(S_RB, S_T, 8, 1)[:, 0].reshape(nb, 1)
        n_ref[:, h, :] = dl_seq * n_old + wlk_sum

    og_ref[...] = _sigmoid(_mm(a, wbig_ref[:, 1024:1536]))
    zg_ref[...] = _silu(_mm(a, wbig_ref[:, 512:1024]))

    u = _mm(a, wbig_ref[:, 1536:2048])
    sre_ref[...] = sre_in_ref[...].T
    sim_ref[...] = sim_in_ref[...].T
    views = _s5_views(bure_s, buim_s, xsre_s, xsim_s)
    _s5_in(views, u, wb_ref)
    for rb in range(S_RB):
        _s5_scan(views, are_ref, aim_ref, sre_ref, sim_ref, rb * S_BLK, slice(rb * 8, (rb + 1) * 8), S_T)
    y_s5 = jnp.concatenate(_s5_out(views, wcre_ref, wcim_ref), axis=-1)
    sre_out_ref[...] = sre_ref[...].T
    sim_out_ref[...] = sim_ref[...].T
    yb_ref[...] = _s5_post(y_s5, u, _mm(a, wbig_ref[:, 2048:2560]), wglu_ref, s5d_ref, bglu_ref)


def _rows_to_blocks(src_ref, dst_s):
    for rb in range(S_RB):
        for t in range(S_T):
            dst_s[rb * S_BLK + t * 8:rb * S_BLK + t * 8 + 8, :] = src_ref[rb * 8:(rb + 1) * 8, t, :]
    return dst_s[...]


def _sample_front_call(x3, weights, vecs, are8, aim8, n3, mrow_in, conv_t, sre_t, sim_t):
    (wbig, wgate, wqk, wv, wb, wcre, wcim, wglu, _, _, _) = weights
    (lnmix, convw, convb, bi8, bf8, _, _, s5d, bglu, _, _) = vecs
    nb = S_RB * 8
    f32 = lambda *s: jax.ShapeDtypeStruct(s, F32)
    bf = lambda *s: jax.ShapeDtypeStruct(s, BF16)
    out_shape = (bf(ROWS, MIX), bf(ROWS, MIX), bf(ROWS, MIX), f32(ROWS, 128), f32(ROWS, MIX), f32(ROWS, MIX),
                 f32(ROWS, MIX), f32(ROWS, MIX), f32(ROWS, MIX), f32(ROWS, MIX), f32(ROWS, MIX),
                 f32(nb, N_HEADS, HEAD_DIM), f32(8, ROWS), f32(CONV_W - 1, nb, MIX), f32(N_STATE, nb),
                 f32(N_STATE, nb))
    scratch = [pltpu.VMEM((S_RB * (CONV_W - 1 + S_T) * 8, MIX), F32),
               pltpu.VMEM((2, ROWS, S5_HALF), F32), pltpu.VMEM((2, ROWS, S5_HALF), F32),
               pltpu.VMEM((2, ROWS, S5_HALF), BF16), pltpu.VMEM((2, ROWS, S5_HALF), BF16),
               pltpu.VMEM((ROWS, D_MODEL), F32), pltpu.VMEM((nb, N_STATE), F32), pltpu.VMEM((nb, N_STATE), F32)]
    return pl.pallas_call(
        _sample_front_body, out_shape=out_shape, scratch_shapes=scratch,
        compiler_params=pltpu.CompilerParams(vmem_limit_bytes=VMEM_LIMIT),
        name="sample_front",
    )(x3, wbig, wgate, wqk, wv, wb, wcre, wcim, wglu, lnmix, convw, convb, bi8, bf8, s5d, bglu, are8, aim8,
      n3, mrow_in, conv_t, sre_t, sim_t)


def _sample_state_body(c_ref, q_ref, k_ref, wlv_ref, colt_ref, qc_ref, cnew_ref):
    sub = lax.broadcasted_iota(jnp.int32, (S_BLK, HEAD_DIM), 0) % 8
    for sb in range(S_SB):
        rows = slice(sb * S_BLK, (sb + 1) * S_BLK)
        for h in range(N_HEADS):
            hs = slice(h * HEAD_DIM, (h + 1) * HEAD_DIM)
            q = q_ref[rows, hs]
            k = k_ref[rows, hs]
            wlv = wlv_ref[rows, hs]
            acc = jnp.zeros((S_BLK, HEAD_DIM), F32)
            for b8 in range(8):
                c_old = c_ref[0, sb * 8 + b8, h]
                sel = sub == b8
                acc = jnp.where(sel, _mm(q, c_old.astype(BF16)), acc)
                upd = lax.dot_general(k, jnp.where(sel, wlv, jnp.zeros_like(wlv)), (((0,), (0,)), ((), ())),
                                      preferred_element_type=F32)
                dl = colt_ref[sb * S_BLK + b8:sb * S_BLK + b8 + 1, COL_DL + h:COL_DL + h + 1]
                cnew_ref[0, sb * 8 + b8, h] = dl * c_old + upd
            qc_ref[rows, hs] = acc


S_SB = 4


def _sample_state_call(c_state, q_bf, k_bf, wlv_bf, colt):
    row_spec = lambda w: pl.BlockSpec((S_SB * S_BLK, w), lambda i: (i, 0))
    c_spec = pl.BlockSpec((1, S_SB * 8, N_HEADS, HEAD_DIM, HEAD_DIM), lambda i: (0, i, 0, 0, 0))
    return pl.pallas_call(
        _sample_state_body,
        grid=(S_RB // S_SB,),
        in_specs=[c_spec, row_spec(MIX), row_spec(MIX), row_spec(MIX), row_spec(128)],
        out_specs=(row_spec(MIX), c_spec),
        out_shape=(jax.ShapeDtypeStruct((ROWS, MIX), F32), jax.ShapeDtypeStruct(c_state.shape, F32)),
        compiler_params=pltpu.CompilerParams(dimension_semantics=("arbitrary",), vmem_limit_bytes=VMEM_LIMIT),
        name="sample_state",
    )(c_state, q_bf, k_bf, wlv_bf, colt)


def _sample_back_body(x_ref, p_ref, num_ref, dec_ref, den_ref, qc_ref, og_ref, zg_ref, xc_ref, yb_ref,
                      wout_ref, wple_ref, wpg_ref, lnhead_ref, skip_ref, lnple_ref, lnfin_ref, y_ref,
                      xtm_s, ptm_s):
    ha = (num_ref[...] + dec_ref[...] * qc_ref[...]) / den_ref[...] * og_ref[...]
    ha = jnp.concatenate([_head_norm(ha[:, h * HEAD_DIM:(h + 1) * HEAD_DIM]) for h in range(N_HEADS)], axis=-1)
    ha = (ha * lnhead_ref[...] + skip_ref[...] * xc_ref[...]) * zg_ref[...]
    e_raw = _mm(_rows_to_blocks(p_ref, ptm_s).astype(BF16), wple_ref[...])
    y = _merge_out(_rows_to_blocks(x_ref, xtm_s), e_raw, ha, yb_ref[...], wout_ref, wpg_ref, lnple_ref, lnfin_ref)
    for rb in range(S_RB):
        for t in range(S_T):
            y_ref[rb * 8:(rb + 1) * 8, t, :] = y[rb * S_BLK + t * 8:rb * S_BLK + t * 8 + 8, :]


def _sample_back_call(x3, p3, num, dec, den, qc, og, zg, xc, yb, weights, vecs):
    (_, _, _, _, _, _, _, _, wout, wple, wpg) = weights
    (_, _, _, _, _, lnhead, skip, _, _, lnple, lnfin) = vecs
    return pl.pallas_call(
        _sample_back_body, out_shape=jax.ShapeDtypeStruct(x3.shape, F32),
        scratch_shapes=[pltpu.VMEM((ROWS, D_MODEL), F32), pltpu.VMEM((ROWS, D_PLE), F32)],
        compiler_params=pltpu.CompilerParams(vmem_limit_bytes=VMEM_LIMIT),
        name="sample_back",
    )(x3, p3, num, dec, den, qc, og, zg, xc, yb, wout, wple, wpg, lnhead, skip, lnple, lnfin)


def _sample_path(x_sample, p_sample, c_state, n_state, m_state, conv_state, sre_state, sim_state,
                 weights, vecs, are8, aim8):
    nb, n_t, _ = x_sample.shape
    assert nb == S_RB * 8 and n_t == S_T
    m_rows = jnp.broadcast_to(jnp.transpose(m_state).reshape(N_HEADS, S_RB, 1, 8), (N_HEADS, S_RB, S_T, 8))
    mrow_in = jnp.pad(m_rows.reshape(N_HEADS, ROWS), ((0, 8 - N_HEADS), (0, 0)))
    s5_t = lambda s: jnp.transpose(s, (1, 2, 0)).reshape(N_STATE, nb)
    (q_bf, k_bf, wlv_bf, colt, num, dec, den, og, zg, xc, yb, n_new, mrow, conv_new, sre, sim) = _sample_front_call(
        x_sample, weights, vecs, are8, aim8, n_state, mrow_in, jnp.transpose(conv_state, (1, 0, 2)),
        s5_t(sre_state), s5_t(sim_state))
    qc, c_new = _sample_state_call(c_state, q_bf, k_bf, wlv_bf, colt)
    y_s = _sample_back_call(x_sample, p_sample, num, dec, den, qc, og, zg, xc, yb, weights, vecs)
    m_new = jnp.transpose(mrow[0:N_HEADS].reshape(N_HEADS, S_RB, S_T, 8)[:, :, 0, :].reshape(N_HEADS, nb))
    s5_back = lambda s: jnp.transpose(s.reshape(N_GROUPS, S5_STATE, nb), (2, 0, 1))[None]
    return (y_s, c_new, n_new[None], m_new[None], jnp.transpose(conv_new, (1, 0, 2))[None],
            s5_back(sre), s5_back(sim))


IN_SPLITS = (0, MIX, 2 * MIX, 3 * MIX + 2 * N_HEADS, 4 * MIX + 2 * N_HEADS)
GATE_ROW = 3 * MIX


def _weight_prep_body(wint_ref, wq_ref, wk_ref, wv_ref, wglu_ref, wout_ref, wple_ref, wpg_ref,
                      wbig_ref, wgate_ref, wqk_ref, wvo_ref, wgluo_ref, wouto_ref, wpleo_ref, wpgo_ref):
    for s, r0 in enumerate(IN_SPLITS):
        wbig_ref[:, s * MIX:(s + 1) * MIX] = wint_ref[r0:r0 + MIX, :].T.astype(BF16)
    g8 = wint_ref[GATE_ROW:GATE_ROW + 8, :]
    row = lax.broadcasted_iota(jnp.int32, g8.shape, 0)
    zero = jnp.zeros_like(g8)
    top = jnp.where(row < N_HEADS, g8, zero)
    bot = jnp.where(row < N_HEADS, pltpu.roll(g8, N_HEADS, axis=0), zero)
    wgate_ref[...] = jnp.concatenate([top, bot], axis=0).astype(BF16)
    for h in range(N_HEADS):
        wqk_ref[h, :, 0:HEAD_DIM] = wq_ref[0, h].astype(BF16)
        wqk_ref[h, :, HEAD_DIM:2 * HEAD_DIM] = wk_ref[0, h].astype(BF16)
    wvo_ref[...] = wv_ref[0].astype(BF16)
    wgluo_ref[...] = wglu_ref[0].astype(BF16)
    wouto_ref[...] = wout_ref[0].astype(BF16)
    wpleo_ref[...] = wple_ref[0].astype(BF16)
    wpgo_ref[...] = wpg_ref[0].astype(BF16)


def _weight_prep_call(w_in_t, w_q, w_k, w_v, w_glu, w_out, w_ple, w_pg):
    bf = lambda *s: jax.ShapeDtypeStruct(s, BF16)
    return pl.pallas_call(
        _weight_prep_body,
        out_shape=(bf(D_MODEL, 5 * MIX), bf(16, D_MODEL), bf(N_HEADS, HEAD_DIM, 2 * HEAD_DIM),
                   bf(N_HEADS, HEAD_DIM, HEAD_DIM), bf(MIX, MIX), bf(2 * MIX, D_MODEL), bf(D_PLE, D_MODEL),
                   bf(D_MODEL, D_MODEL)),
        compiler_params=pltpu.CompilerParams(vmem_limit_bytes=VMEM_LIMIT),
        name="weight_prep",
    )(w_in_t, w_q, w_k, w_v, w_glu, w_out, w_ple, w_pg)


def _layout_params(ln_mix, w_in, b_igate, b_fgate, conv_w, conv_b, w_q, w_k, w_v, ln_head, skip_a,
                   s5_lam_re, s5_lam_im, s5_log_dt, s5_B_re, s5_B_im, s5_C_re, s5_C_im, s5_D, w_glu, b_glu,
                   w_out, w_ple, ln_ple, w_ple_gate, ln_final):
    wbig, wgate, wqk, wv, wglu_bf, wout_bf, wple_bf, wpg_bf = _weight_prep_call(
        jnp.transpose(w_in[0]), w_q, w_k, w_v, w_glu, w_out, w_ple, w_ple_gate)
    b_gcp = lambda m: jnp.transpose(m[0], (0, 2, 1)).reshape(2, 16 * S5_GROUP, S5_STATE)
    c_gcp = lambda m: m[0].reshape(2, 16 * S5_GROUP, S5_STATE)
    lam = jnp.concatenate([v.reshape(1, N_STATE) for v in (s5_lam_re, s5_lam_im, s5_log_dt)], axis=0)
    are8, aim8, wb, wcre, wcim = _prep_call(lam, b_gcp(s5_B_re), b_gcp(s5_B_im), c_gcp(s5_C_re), c_gcp(s5_C_im))
    weights = [wbig, wgate, wqk, wv, wb, wcre, wcim, wglu_bf, wout_bf, wple_bf, wpg_bf]
    pad8 = lambda v: jnp.pad(v[0][:, None], ((0, 8 - N_HEADS), (0, 0)))
    vecs = [ln_mix[0][None, :], conv_w[0], conv_b[0][None, :], pad8(b_igate), pad8(b_fgate),
            ln_head[0][None, :], skip_a[0][None, :], s5_D[0][None, :], b_glu[0][None, :],
            ln_ple[0][None, :], ln_final[None, :]]
    return weights, vecs, are8, aim8


def kernel(x_prompt, x_sample, p_prompt, p_sample, state_mlstm_C, state_mlstm_n, state_mlstm_m, state_conv, state_s5_re, state_s5_im, ln_mix, w_in, b_igate, b_fgate, conv_w, conv_b, w_q, w_k, w_v, ln_head, skip_a, s5_lam_re, s5_lam_im, s5_log_dt, s5_B_re, s5_B_im, s5_C_re, s5_C_im, s5_D, w_glu, b_glu, w_out, w_ple, ln_ple, w_ple_gate, ln_final):
    weights, vecs, are8, aim8 = _layout_params(
        ln_mix, w_in, b_igate, b_fgate, conv_w, conv_b, w_q, w_k, w_v, ln_head, skip_a, s5_lam_re, s5_lam_im,
        s5_log_dt, s5_B_re, s5_B_im, s5_C_re, s5_C_im, s5_D, w_glu, b_glu, w_out, w_ple, ln_ple, w_ple_gate,
        ln_final)

    nbp = x_prompt.shape[0]
    y_p, c_p, n_p, mrow_p, conv_t, sre_p, sim_p = _prompt_call(x_prompt, p_prompt[0], weights, vecs, are8, aim8)
    conv_p = jnp.transpose(conv_t, (1, 0, 2))[None]
    pn = n_p.reshape(1, nbp, N_HEADS, HEAD_DIM)
    pm = jnp.transpose(mrow_p[0:N_HEADS, ::P_T])[None]
    pre = sre_p.reshape(1, nbp, N_GROUPS, S5_STATE)
    pim = sim_p.reshape(1, nbp, N_GROUPS, S5_STATE)

    y_s, c_s, n_s, m_s, conv_s, sre_s, sim_s = _sample_path(
        x_sample, p_sample[0], state_mlstm_C, state_mlstm_n[0], state_mlstm_m[0], state_conv[0],
        state_s5_re[0], state_s5_im[0], weights, vecs, are8, aim8)
    return (y_p, y_s, c_p, pn, pm, conv_p, pre, pim, c_s, n_s, m_s, conv_s, sre_s, sim_s)
```
